```python
import math
import jax
import jax.numpy as jnp
from jax import lax
import numpy as np

D_MODEL = 2048
BATCH = 8
SEQ = 2048
DEPTH = 1
DEC_BATCH = 128
DEC_SEQ = 1
PAST_LEN = 16384
PAGE_SIZE = 128

N_META = 16
N_HEADS = 16
Q_LORA = 512
KV_LORA = 512
D_NOPE = 128
D_ROPE = 64
D_V = 128
D_QK = D_NOPE + D_ROPE
ROPE_THETA = 10000.0
D_CONV = D_MODEL
CONV_W = 3
N_EXPERTS = 32
TOP_K = 4
D_FF = D_MODEL
SWIGLU_LIMIT = 7.0
SWIGLU_ALPHA = 1.702
Q_BLOCK = 128
LN_EPS = 1e-5
RMS_EPS = 1e-6
NEG_INF = -1e30
DN_ALPHA = (2 * DEPTH) ** 0.25
DN_BETA = (8 * DEPTH) ** -0.25
IN_SIZES = (Q_LORA, KV_LORA, D_ROPE, D_CONV, D_CONV, D_CONV, D_MODEL, D_MODEL)
IN_OFFSETS = tuple(int(v) for v in np.cumsum(IN_SIZES)[:-1])
N_IN = int(sum(IN_SIZES))

kernel_name = "mla_shortconv_gated_moe_deepnorm_step"


def layer_norm(x, g, b):
    xf = x.astype(jnp.float32)
    mu = jnp.mean(xf, axis=-1, keepdims=True)
    var = jnp.mean(jnp.square(xf - mu), axis=-1, keepdims=True)
    y = (xf - mu) * lax.rsqrt(var + LN_EPS) * g.astype(jnp.float32) + b.astype(jnp.float32)
    return y.astype(x.dtype)


def rms_norm(x, g):
    xf = x.astype(jnp.float32)
    y = xf * lax.rsqrt(jnp.mean(jnp.square(xf), axis=-1, keepdims=True) + RMS_EPS) * g.astype(jnp.float32)
    return y.astype(x.dtype)


def rope(x, pos):
    half = D_ROPE // 2
    inv_freq = ROPE_THETA ** (-jnp.arange(half, dtype=jnp.float32) / half)
    ang = pos.astype(jnp.float32)[:, None] * inv_freq[None, :]
    bshape = (1, pos.shape[0]) + (1,) * (x.ndim - 3) + (half,)
    cos = jnp.cos(ang).reshape(bshape)
    sin = jnp.sin(ang).reshape(bshape)
    xf = x.astype(jnp.float32)
    x1, x2 = xf[..., :half], xf[..., half:]
    return jnp.concatenate([x1 * cos - x2 * sin, x2 * cos + x1 * sin], axis=-1).astype(x.dtype)


def in_projection(x, pos, w_in, g_q, w_q_up, g_kv):
    bsz, s = x.shape[0], x.shape[1]
    q_lat, kv_lat, k_pe, b_gate, c_gate, h_in, g_att, g_cnv = jnp.split(x @ w_in, IN_OFFSETS, axis=-1)
    q = (rms_norm(q_lat, g_q) @ w_q_up).reshape(bsz, s, N_HEADS, D_QK)
    q_nope = q[..., :D_NOPE]
    q_pe = rope(q[..., D_NOPE:], pos)
    c_kv = rms_norm(kv_lat, g_kv)
    k_pe = rope(k_pe, pos)
    u = c_gate * h_in
    return q_nope, q_pe, c_kv, k_pe, u, b_gate, g_att, g_cnv


def causal_conv(u_ext, w):
    s = u_ext.shape[1] - (CONV_W - 1)
    y = w[0] * u_ext[:, 0:s]
    for j in range(1, CONV_W):
        y = y + w[j] * u_ext[:, j:j + s]
    return y


def prompt_attention(q_nope, q_pe, c_kv, k_pe, w_kv_up):
    bsz, L = q_nope.shape[0], q_nope.shape[1]
    kv = (c_kv @ w_kv_up).reshape(bsz, L, N_HEADS, D_NOPE + D_V)
    k_nope, v = kv[..., :D_NOPE], kv[..., D_NOPE:]
    k = jnp.concatenate([k_nope, jnp.broadcast_to(k_pe[:, :, None, :], (bsz, L, N_HEADS, D_ROPE))], axis=-1)
    q = jnp.concatenate([q_nope, q_pe], axis=-1)
    nb = -(-L // Q_BLOCK)
    lp = nb * Q_BLOCK
    q = jnp.pad(q, ((0, 0), (0, lp - L), (0, 0), (0, 0)))
    q = q.reshape(bsz, nb, Q_BLOCK, N_HEADS, D_QK).transpose(1, 0, 2, 3, 4)
    kpos = jnp.arange(L)
    scale = 1.0 / math.sqrt(D_QK)

    def block(args):
        qb, i = args
        qpos = i * Q_BLOCK + jnp.arange(Q_BLOCK)
        s = jnp.einsum('bqhd,bkhd->bhqk', qb, k).astype(jnp.float32) * scale
        s = jnp.where(kpos[None, :] <= qpos[:, None], s, NEG_INF)
        p = jax.nn.softmax(s, axis=-1).astype(v.dtype)
        return jnp.einsum('bhqk,bkhd->bqhd', p, v)

    o = lax.map(block, (q, jnp.arange(nb)))
    return o.transpose(1, 0, 2, 3, 4).reshape(bsz, lp, N_HEADS, D_V)[:, :L]


def sample_attention(q_nope, q_pe, c_kv, k_pe, w_kv_up, cache_kv_latent, cache_k_rope, page_table, d):
    s_new = q_nope.shape[1]
    w = w_kv_up.reshape(KV_LORA, N_HEADS, D_NOPE + D_V)
    w_uk, w_uv = w[..., :D_NOPE], w[..., D_NOPE:]
    q_abs = jnp.einsum('bshn,rhn->bshr', q_nope, w_uk)
    scale = 1.0 / math.sqrt(D_QK)
    tri = jnp.arange(s_new)[None, :] <= jnp.arange(s_new)[:, None]

    def one(args):
        pages, qa, qp, ckv_new, kpe_new = args
        ckv = jnp.concatenate([cache_kv_latent[d, pages].reshape(-1, KV_LORA).astype(ckv_new.dtype), ckv_new], axis=0)
        kpe = jnp.concatenate([cache_k_rope[d, pages].reshape(-1, D_ROPE).astype(kpe_new.dtype), kpe_new], axis=0)
        n_past = ckv.shape[0] - s_new
        s = (jnp.einsum('shr,kr->shk', qa, ckv) + jnp.einsum('shd,kd->shk', qp, kpe)).astype(jnp.float32) * scale
        mask = jnp.concatenate([jnp.ones((s_new, n_past), dtype=bool), tri], axis=1)
        s = jnp.where(mask[:, None, :], s, NEG_INF)
        p = jax.nn.softmax(s, axis=-1).astype(ckv.dtype)
        return jnp.einsum('shk,kr->shr', p, ckv)

    o_lat = lax.map(one, (page_table, q_abs, q_pe, c_kv, k_pe))
    return jnp.einsum('bshr,rhv->bshv', o_lat, w_uv)


def moe(h, router_w, router_b, w_gate_up, b_gate_up, w_down, b_down):
    shp = h.shape
    t = h.reshape(-1, shp[-1])
    logits = (t @ router_w + router_b).astype(jnp.float32)
    top_v, top_i = lax.top_k(logits, TOP_K)
    wts = jax.nn.softmax(top_v, axis=-1)
    comb = jnp.sum(jax.nn.one_hot(top_i, N_EXPERTS, dtype=jnp.float32) * wts[..., None], axis=1).astype(t.dtype)
    out = jnp.zeros_like(t)
    for e in range(N_EXPERTS):
        gu = t @ w_gate_up[e] + b_gate_up[e]
        gate = jnp.minimum(gu[:, :D_FF], SWIGLU_LIMIT)
        up = jnp.clip(gu[:, D_FF:], -SWIGLU_LIMIT, SWIGLU_LIMIT)
        y = ((up + 1.0) * (gate * jax.nn.sigmoid(SWIGLU_ALPHA * gate))) @ w_down[e] + b_down[e]
        out = out + comb[:, e:e + 1] * y
    return out.reshape(shp)


def merge_and_channel_mix(x, att, conv_y, b_gate, g_att, g_cnv, w_attn_proj, w_conv_proj, w_out,
                          ln1_g, ln1_b, router_w, router_b, w_gate_up, b_gate_up, w_down, b_down, ln2_g, ln2_b):
    bsz, s = x.shape[0], x.shape[1]
    a = att.reshape(bsz, s, N_HEADS * D_V) @ w_attn_proj
    c = (b_gate * conv_y) @ w_conv_proj
    m = jax.nn.sigmoid(g_att) * a + jax.nn.sigmoid(g_cnv) * c
    h = layer_norm(DN_ALPHA * x + m @ w_out, ln1_g, ln1_b)
    return layer_norm(DN_ALPHA * h + moe(h, router_w, router_b, w_gate_up, b_gate_up, w_down, b_down), ln2_g, ln2_b)


def setup_inputs(seed: int = 0) -> dict:
    key = jax.random.key(seed)
    ks = jax.random.split(key, 32)
    f32 = jnp.float32

    def nrm(k, shape, scale):
        return jax.random.normal(k, shape, f32) * scale

    n_pages = PAST_LEN // PAGE_SIZE
    n_used = DEC_BATCH * n_pages
    n_pool = n_used + max(1, n_used // 4)
    page_table = jax.random.permutation(ks[5], n_pool)[:n_used].reshape(DEC_BATCH, n_pages).astype(jnp.int32)
    return {
        "x_prompt": nrm(ks[0], (BATCH, SEQ, D_MODEL), 1.0),
        "x_sample": nrm(ks[1], (DEC_BATCH, DEC_SEQ, D_MODEL), 1.0),
        "cache_kv_latent": nrm(ks[2], (DEPTH, n_pool, PAGE_SIZE, KV_LORA), 1.0),
        "cache_k_rope": nrm(ks[3], (DEPTH, n_pool, PAGE_SIZE, D_ROPE), 1.0),
        "state_conv": nrm(ks[4], (DEPTH, DEC_BATCH, CONV_W - 1, D_CONV), 1.0),
        "page_table": page_table,
        "meta_tokens": nrm(ks[6], (N_META, D_MODEL), 1.0),
        "ln0_g": 1.0 + nrm(ks[7], (D_MODEL,), 0.01),
        "ln0_b": nrm(ks[8], (D_MODEL,), 0.01),
        "w_in": nrm(ks[9], (DEPTH, D_MODEL, N_IN), D_MODEL ** -0.5),
        "g_q_norm": 1.0 + nrm(ks[10], (DEPTH, Q_LORA), 0.01),
        "w_q_up": nrm(ks[11], (DEPTH, Q_LORA, N_HEADS * D_QK), Q_LORA ** -0.5),
        "g_kv_norm": 1.0 + nrm(ks[12], (DEPTH, KV_LORA), 0.01),
        "w_kv_up": nrm(ks[13], (DEPTH, KV_LORA, N_HEADS * (D_NOPE + D_V)), KV_LORA ** -0.5),
        "w_attn_proj": nrm(ks[14], (DEPTH, N_HEADS * D_V, D_MODEL), (N_HEADS * D_V) ** -0.5),
        "conv_w": nrm(ks[15], (DEPTH, CONV_W, D_CONV), CONV_W ** -0.5),
        "w_conv_proj": nrm(ks[16], (DEPTH, D_CONV, D_MODEL), D_CONV ** -0.5),
        "w_out": nrm(ks[17], (DEPTH, D_MODEL, D_MODEL), D_MODEL ** -0.5 * DN_BETA),
        "ln1_g": 1.0 + nrm(ks[18], (DEPTH, D_MODEL), 0.01),
        "ln1_b": nrm(ks[19], (DEPTH, D_MODEL), 0.01),
        "router_w": nrm(ks[20], (DEPTH, D_MODEL, N_EXPERTS), D_MODEL ** -0.5),
        "router_b": nrm(ks[21], (DEPTH, N_EXPERTS), 0.01),
        "w_gate_up": nrm(ks[22], (DEPTH, N_EXPERTS, D_MODEL, 2 * D_FF), D_MODEL ** -0.5),
        "b_gate_up": nrm(ks[23], (DEPTH, N_EXPERTS, 2 * D_FF), 0.01),
        "w_down": nrm(ks[24], (DEPTH, N_EXPERTS, D_FF, D_MODEL), D_FF ** -0.5 * DN_BETA),
        "b_down": nrm(ks[25], (DEPTH, N_EXPERTS, D_MODEL), 0.01),
        "ln2_g": 1.0 + nrm(ks[26], (DEPTH, D_MODEL), 0.01),
        "ln2_b": nrm(ks[27], (DEPTH, D_MODEL), 0.01),
    }


def reference(x_prompt, x_sample, cache_kv_latent, cache_k_rope, state_conv, page_table, meta_tokens,
              ln0_g, ln0_b, w_in, g_q_norm, w_q_up, g_kv_norm, w_kv_up, w_attn_proj, conv_w, w_conv_proj,
              w_out, ln1_g, ln1_b, router_w, router_b, w_gate_up, b_gate_up, w_down, b_down, ln2_g, ln2_b):
    bsz = x_prompt.shape[0]
    meta = jnp.broadcast_to(meta_tokens[None].astype(x_prompt.dtype), (bsz, N_META, x_prompt.shape[-1]))
    xp = layer_norm(jnp.concatenate([meta, x_prompt], axis=1), ln0_g, ln0_b)
    xs = layer_norm(x_sample, ln0_g, ln0_b)
    pos_p = jnp.arange(xp.shape[1])
    pos_s = PAST_LEN + jnp.arange(xs.shape[1])
    ckv_p, kpe_p, conv_p, ckv_s, kpe_s, conv_s = [], [], [], [], [], []
    for d in range(DEPTH):
        qn, qp, ckv, kpe, u, bg, ga, gc = in_projection(xp, pos_p, w_in[d], g_q_norm[d], w_q_up[d], g_kv_norm[d])
        att = prompt_attention(qn, qp, ckv, kpe, w_kv_up[d])
        u_ext = jnp.pad(u, ((0, 0), (CONV_W - 1, 0), (0, 0)))
        cy = causal_conv(u_ext, conv_w[d])
        ckv_p.append(ckv)
        kpe_p.append(kpe)
        conv_p.append(u_ext[:, -(CONV_W - 1):])
        xp = merge_and_channel_mix(xp, att, cy, bg, ga, gc, w_attn_proj[d], w_conv_proj[d], w_out[d],
                                   ln1_g[d], ln1_b[d], router_w[d], router_b[d], w_gate_up[d], b_gate_up[d],
                                   w_down[d], b_down[d], ln2_g[d], ln2_b[d])
        qn, qp, ckv, kpe, u, bg, ga, gc = in_projection(xs, pos_s, w_in[d], g_q_norm[d], w_q_up[d], g_kv_norm[d])
        att = sample_attention(qn, qp, ckv, kpe, w_kv_up[d], cache_kv_latent, cache_k_rope, page_table, d)
        u_ext = jnp.concatenate([state_conv[d].astype(u.dtype), u], axis=1)
        cy = causal_conv(u_ext, conv_w[d])
        ckv_s.append(ckv)
        kpe_s.append(kpe)
        conv_s.append(u_ext[:, -(CONV_W - 1):])
        xs = merge_and_channel_mix(xs, att, cy, bg, ga, gc, w_attn_proj[d], w_conv_proj[d], w_out[d],
                                   ln1_g[d], ln1_b[d], router_w[d], router_b[d], w_gate_up[d], b_gate_up[d],
                                   w_down[d], b_down[d], ln2_g[d], ln2_b[d])
    y_prompt = xp[:, N_META:]
    return (y_prompt, xs, jnp.stack(ckv_p), jnp.stack(kpe_p), jnp.stack(conv_p),
            jnp.stack(ckv_s), jnp.stack(kpe_s), jnp.stack(conv_s))
```

```python
import functools
import math

import jax
import jax.numpy as jnp
from jax import lax
from jax.experimental import pallas as pl
from jax.experimental.pallas import tpu as pltpu

F32 = jnp.float32
BF16 = jnp.bfloat16

N_HEADS = 16
D_NOPE = 128
D_ROPE = 64
D_V = 128
D_QK = D_NOPE + D_ROPE
ROPE_THETA = 10000.0
CONV_W = 3
TOP_K = 4
SWIGLU_LIMIT = 7.0
SWIGLU_ALPHA = 1.702
LN_EPS = 1e-5
RMS_EPS = 1e-6
NEG_INF = -1e30

LANES = 128
SUBLANES = 8
BF16_ROWS = 16
VMEM_LIMIT = 56 * 1024 * 1024

ROW_CAP = 704
CONV_COL_CAP = 512
MERGE_COL_CAP = 1024
ATTN_Q_ROWS = 512
PAGES_PER_STEP = 16
EXPERT_ROWS = 256
FF_COL_CAP = 1024
GATHER_ROWS = 128


def _pick(n, cap, mult):
    best = None
    for c in range(mult, min(n, cap) + 1, mult):
        if n % c == 0:
            best = c
    if best is None:
        raise ValueError(f"no block of {n} that is a multiple of {mult} and <= {cap}")
    return best


def _cparams(sem):
    return pltpu.CompilerParams(dimension_semantics=sem, vmem_limit_bytes=VMEM_LIMIT)


def _layer_norm(x, g, b):
    mu = jnp.mean(x, axis=-1, keepdims=True)
    xc = x - mu
    var = jnp.mean(xc * xc, axis=-1, keepdims=True)
    return xc * lax.rsqrt(var + LN_EPS) * g + b


def _rms_norm(x, g):
    return x * lax.rsqrt(jnp.mean(x * x, axis=-1, keepdims=True) + RMS_EPS) * g


def _dot(a, b):
    return jnp.dot(a, b, preferred_element_type=F32)


def _dot_nt(a, b):
    return lax.dot_general(a, b, (((1,), (1,)), ((), ())), preferred_element_type=F32)


def _ln_in_proj_kernel(x_ref, g0_ref, b0_ref, wa_ref, gq_ref, gkv_ref, c_ref, s_ref,
                       xn_ref, ckv_ref, kpe_ref, qn_ref, ckvb_ref, *, ql, kvl):
    xn = _layer_norm(x_ref[...], g0_ref[...], b0_ref[...]).astype(BF16)
    xn_ref[...] = xn
    za = _dot(xn, wa_ref[...])
    ckv = _rms_norm(za[:, ql:ql + kvl], gkv_ref[...])
    ckv_ref[...] = ckv
    ckvb_ref[...] = ckv.astype(BF16)
    o = ql + kvl
    kpe_ref[...] = za[:, o:o + LANES] * c_ref[...] + za[:, o + LANES:o + 2 * LANES] * s_ref[...]
    qn_ref[...] = _rms_norm(za[:, :ql], gq_ref[...]).astype(BF16)


def _ln_in_proj(x, ln0_g, ln0_b, wa, gq, gkv, c128, s128, *, seq):
    tp, d = x.shape
    ql, kvl = gq.shape[1], gkv.shape[1]
    tm = _pick(seq, ROW_CAP, BF16_ROWS)
    nsb = seq // tm
    row = lambda i: (i, 0)
    const = lambda i: (0, 0)
    return pl.pallas_call(
        functools.partial(_ln_in_proj_kernel, ql=ql, kvl=kvl),
        grid=(tp // tm,),
        in_specs=[
            pl.BlockSpec((tm, d), row),
            pl.BlockSpec((1, d), const),
            pl.BlockSpec((1, d), const),
            pl.BlockSpec(wa.shape, const),
            pl.BlockSpec((1, ql), const),
            pl.BlockSpec((1, kvl), const),
            pl.BlockSpec((tm, LANES), lambda i: (i % nsb, 0)),
            pl.BlockSpec((tm, LANES), lambda i: (i % nsb, 0)),
        ],
        out_specs=[
            pl.BlockSpec((tm, d), row),
            pl.BlockSpec((tm, kvl), row),
            pl.BlockSpec((tm, LANES), row),
            pl.BlockSpec((tm, ql), row),
            pl.BlockSpec((tm, kvl), row),
        ],
        out_shape=[
            jax.ShapeDtypeStruct((tp, d), BF16),
            jax.ShapeDtypeStruct((tp, kvl), F32),
            jax.ShapeDtypeStruct((tp, LANES), F32),
            jax.ShapeDtypeStruct((tp, ql), BF16),
            jax.ShapeDtypeStruct((tp, kvl), BF16),
        ],
        compiler_params=_cparams(("parallel",)),
        name="ln_in_proj",
    )(x, ln0_g, ln0_b, wa, gq, gkv, c128, s128)


def _sample_in_proj_kernel(x_ref, g0_ref, b0_ref, wa_ref, gq_ref, gkv_ref, c_ref, s_ref,
                           wq_ref, wqr_ref, wuk_ref,
                           xn_ref, ckv_ref, kpe_ref, qabs_ref, qpe_ref, *, ql, kvl, n_heads):
    xn = _layer_norm(x_ref[...], g0_ref[...], b0_ref[...]).astype(BF16)
    xn_ref[...] = xn
    za = _dot(xn, wa_ref[...])
    ckv_ref[...] = _rms_norm(za[:, ql:ql + kvl], gkv_ref[...])
    o = ql + kvl
    c = c_ref[...]
    s = s_ref[...]
    kpe_ref[...] = za[:, o:o + LANES] * c + za[:, o + LANES:o + 2 * LANES] * s
    qn = _rms_norm(za[:, :ql], gq_ref[...]).astype(BF16)
    for h in range(n_heads):
        qa = _dot(qn, wq_ref[h])
        qb = _dot(qn, wqr_ref[h])
        qpe_ref[h] = (qa[:, D_NOPE:] * c + qb * s).astype(BF16)
        qabs_ref[h] = _dot(qa[:, :D_NOPE].astype(BF16), wuk_ref[h]).astype(BF16)


def _sample_in_proj(x, ln0_g, ln0_b, wa, gq, gkv, c128, s128, wq3, wqr3, wuk3):
    db, d = x.shape
    ql, kvl = gq.shape[1], gkv.shape[1]
    h = wq3.shape[0]
    const2 = lambda i: (0, 0)
    const3 = lambda i: (0, 0, 0)
    return pl.pallas_call(
        functools.partial(_sample_in_proj_kernel, ql=ql, kvl=kvl, n_heads=h),
        grid=(1,),
        in_specs=[
            pl.BlockSpec((db, d), const2),
            pl.BlockSpec((1, d), const2),
            pl.BlockSpec((1, d), const2),
            pl.BlockSpec(wa.shape, const2),
            pl.BlockSpec((1, ql), const2),
            pl.BlockSpec((1, kvl), const2),
            pl.BlockSpec((1, LANES), const2),
            pl.BlockSpec((1, LANES), const2),
            pl.BlockSpec(wq3.shape, const3),
            pl.BlockSpec(wqr3.shape, const3),
            pl.BlockSpec(wuk3.shape, const3),
        ],
        out_specs=[
            pl.BlockSpec((db, d), const2),
            pl.BlockSpec((db, kvl), const2),
            pl.BlockSpec((db, LANES), const2),
            pl.BlockSpec((h, db, kvl), const3),
            pl.BlockSpec((h, db, LANES), const3),
        ],
        out_shape=[
            jax.ShapeDtypeStruct((db, d), BF16),
            jax.ShapeDtypeStruct((db, kvl), F32),
            jax.ShapeDtypeStruct((db, LANES), F32),
            jax.ShapeDtypeStruct((h, db, kvl), BF16),
            jax.ShapeDtypeStruct((h, db, LANES), BF16),
        ],
        compiler_params=_cparams(("arbitrary",)),
        name="sample_in_proj",
    )(x, ln0_g, ln0_b, wa, gq, gkv, c128, s128, wq3, wqr3, wuk3)


def _conv_gate_prompt_kernel(xn_ref, w_ref, cw_ref, yb_ref, sa_ref, sc_ref, cst_ref, halo_ref, *, nsb):
    i = pl.program_id(1)
    xn = xn_ref[...]
    tm = xn.shape[0]
    u = _dot(xn, w_ref[1]) * _dot(xn, w_ref[2])

    @pl.when((i % nsb) == 0)
    def _():
        halo_ref[...] = jnp.zeros(halo_ref.shape, F32)

    prev = halo_ref[...]
    p1 = prev[SUBLANES - 1:SUBLANES]
    p2 = prev[SUBLANES - 2:SUBLANES - 1]
    row = lax.broadcasted_iota(jnp.int32, (tm, 1), 0)
    u1 = jnp.where(row == 0, p1, pltpu.roll(u, 1, 0))
    u2 = jnp.where(row == 0, p2, jnp.where(row == 1, p1, pltpu.roll(u, 2, 0)))
    cw = cw_ref[...]
    y = cw[0:1] * u2 + cw[1:2] * u1 + cw[2:3] * u
    halo_ref[...] = u[tm - SUBLANES:]
    cst_ref[0] = u[tm - (CONV_W - 1):]
    yb_ref[...] = (_dot(xn, w_ref[0]) * y).astype(BF16)
    sa_ref[...] = jax.nn.sigmoid(_dot(xn, w_ref[3]))
    sc_ref[...] = jax.nn.sigmoid(_dot(xn, w_ref[4]))


def _conv_gate_prompt(xn, w5, conv_w, *, seq):
    tp, d = xn.shape
    dc = w5.shape[2]
    tm = _pick(seq, ROW_CAP, BF16_ROWS)
    tn = _pick(dc, CONV_COL_CAP, LANES)
    nsb = seq // tm
    nb = tp // seq
    blk = lambda j, i: (i, j)
    return pl.pallas_call(
        functools.partial(_conv_gate_prompt_kernel, nsb=nsb),
        grid=(dc // tn, tp // tm),
        in_specs=[
            pl.BlockSpec((tm, d), lambda j, i: (i, 0)),
            pl.BlockSpec((5, d, tn), lambda j, i: (0, 0, j)),
            pl.BlockSpec((CONV_W, tn), lambda j, i: (0, j)),
        ],
        out_specs=[
            pl.BlockSpec((tm, tn), blk),
            pl.BlockSpec((tm, tn), blk),
            pl.BlockSpec((tm, tn), blk),
            pl.BlockSpec((1, CONV_W - 1, tn), lambda j, i: (i // nsb, 0, j)),
        ],
        out_shape=[
            jax.ShapeDtypeStruct((tp, dc), BF16),
            jax.ShapeDtypeStruct((tp, dc), F32),
            jax.ShapeDtypeStruct((tp, dc), F32),
            jax.ShapeDtypeStruct((nb, CONV_W - 1, dc), F32),
        ],
        scratch_shapes=[pltpu.VMEM((SUBLANES, tn), F32)],
        compiler_params=_cparams(("parallel", "arbitrary")),
        name="conv_gate_prompt",
    )(xn, w5, conv_w)


def _conv_gate_sample_kernel(xn_ref, w_ref, cw_ref, s0_ref, s1_ref, yb_ref, sa_ref, sc_ref, u_ref):
    xn = xn_ref[...]
    u = _dot(xn, w_ref[1]) * _dot(xn, w_ref[2])
    cw = cw_ref[...]
    y = cw[0:1] * s0_ref[...] + cw[1:2] * s1_ref[...] + cw[2:3] * u
    u_ref[...] = u
    yb_ref[...] = (_dot(xn, w_ref[0]) * y).astype(BF16)
    sa_ref[...] = jax.nn.sigmoid(_dot(xn, w_ref[3]))
    sc_ref[...] = jax.nn.sigmoid(_dot(xn, w_ref[4]))


def _conv_gate_sample(xn, w5, conv_w, st0, st1):
    db, d = xn.shape
    dc = w5.shape[2]
    tn = _pick(dc, CONV_COL_CAP, LANES)
    col = lambda j: (0, j)
    return pl.pallas_call(
        _conv_gate_sample_kernel,
        grid=(dc // tn,),
        in_specs=[
            pl.BlockSpec((db, d), lambda j: (0, 0)),
            pl.BlockSpec((5, d, tn), lambda j: (0, 0, j)),
            pl.BlockSpec((CONV_W, tn), col),
            pl.BlockSpec((db, tn), col),
            pl.BlockSpec((db, tn), col),
        ],
        out_specs=[pl.BlockSpec((db, tn), col)] * 4,
        out_shape=[
            jax.ShapeDtypeStruct((db, dc), BF16),
            jax.ShapeDtypeStruct((db, dc), F32),
            jax.ShapeDtypeStruct((db, dc), F32),
            jax.ShapeDtypeStruct((db, dc), F32),
        ],
        compiler_params=_cparams(("parallel",)),
        name="conv_gate_sample",
    )(xn, w5, conv_w, st0, st1)


def _prompt_attn_kernel(qn_ref, ckvb_ref, kpe_ref, c_ref, s_ref, wq_ref, wqr_ref, wk_ref, wv_ref,
                        att_ref, q_s, k_s, v_s, *, tq, scale):
    seq = qn_ref.shape[0]
    qn = qn_ref[...]
    qa = _dot(qn, wq_ref[0])
    qb = _dot(qn, wqr_ref[0])
    q_s[:, :D_NOPE] = qa[:, :D_NOPE].astype(BF16)
    q_s[:, D_NOPE:] = (qa[:, D_NOPE:] * c_ref[...] + qb * s_ref[...]).astype(BF16)
    ckvb = ckvb_ref[...]
    k_s[:, :D_NOPE] = _dot(ckvb, wk_ref[0]).astype(BF16)
    k_s[:, D_NOPE:] = kpe_ref[...].astype(BF16)
    v_s[...] = _dot(ckvb, wv_ref[0]).astype(BF16)
    for q0 in range(0, seq, tq):
        q1 = min(q0 + tq, seq)
        s = _dot_nt(q_s[q0:q1, :], k_s[:q1, :]) * scale
        qpos = q0 + lax.broadcasted_iota(jnp.int32, (q1 - q0, 1), 0)
        kpos = lax.broadcasted_iota(jnp.int32, (1, q1), 1)
        s = jnp.where(kpos <= qpos, s, NEG_INF)
        p = jnp.exp(s - jnp.max(s, axis=-1, keepdims=True))
        l = jnp.sum(p, axis=-1, keepdims=True)
        o = _dot(p.astype(BF16), v_s[:q1, :])
        att_ref[q0:q1, :] = (o / l).astype(BF16)


def _prompt_attn(qn, ckvb, kpe128, c128, s128, wq3, wqr3, wk3, wv3, *, seq):
    tp, ql = qn.shape
    kvl = ckvb.shape[1]
    h = wq3.shape[0]
    nb = tp // seq
    scale = 1.0 / math.sqrt(D_QK)
    per_b = lambda b, hh: (b, 0)
    const = lambda b, hh: (0, 0)
    per_h = lambda b, hh: (hh, 0, 0)
    return pl.pallas_call(
        functools.partial(_prompt_attn_kernel, tq=ATTN_Q_ROWS, scale=scale),
        grid=(nb, h),
        in_specs=[
            pl.BlockSpec((seq, ql), per_b),
            pl.BlockSpec((seq, kvl), per_b),
            pl.BlockSpec((seq, LANES), per_b),
            pl.BlockSpec((seq, LANES), const),
            pl.BlockSpec((seq, LANES), const),
            pl.BlockSpec((1, ql, 2 * LANES), per_h),
            pl.BlockSpec((1, ql, LANES), per_h),
            pl.BlockSpec((1, kvl, D_NOPE), per_h),
            pl.BlockSpec((1, kvl, D_V), per_h),
        ],
        out_specs=pl.BlockSpec((seq, D_V), lambda b, hh: (b, hh)),
        out_shape=jax.ShapeDtypeStruct((tp, h * D_V), BF16),
        scratch_shapes=[
            pltpu.VMEM((seq, 2 * LANES), BF16),
            pltpu.VMEM((seq, 2 * LANES), BF16),
            pltpu.VMEM((seq, D_V), BF16),
        ],
        compiler_params=_cparams(("parallel", "arbitrary")),
        name="prompt_attn",
    )(qn, ckvb, kpe128, c128, s128, wq3, wqr3, wk3, wv3)


def _sample_attn_kernel(pt_ref, qa_ref, qp_ref, cn_ref, kn_ref, *rest, npg, scale):
    del pt_ref
    ckv_refs = rest[:npg]
    kpe_refs = rest[npg:2 * npg]
    o_ref = rest[2 * npg]
    m_s, l_s, acc_s = rest[2 * npg + 1:]
    c = pl.program_id(1)
    qa = qa_ref[0]
    qp = qp_ref[0][:, :D_ROPE]

    @pl.when(c == 0)
    def _():
        cn = cn_ref[0].astype(BF16).astype(F32)
        kn = kn_ref[0][:, :D_ROPE].astype(BF16).astype(F32)
        s0 = (jnp.sum(qa.astype(F32) * cn, axis=-1, keepdims=True)
              + jnp.sum(qp.astype(F32) * kn, axis=-1, keepdims=True)) * scale
        m_s[...] = jnp.broadcast_to(s0, m_s.shape)
        l_s[...] = jnp.ones(l_s.shape, F32)
        acc_s[...] = jnp.broadcast_to(cn, acc_s.shape)

    ss = []
    for p in range(npg):
        ck = ckv_refs[p][0].astype(BF16)
        kp = kpe_refs[p][0].astype(BF16)
        ss.append((_dot_nt(qa, ck) + _dot_nt(qp, kp)) * scale)
    m_prev = m_s[...]
    m_cur = jnp.max(functools.reduce(jnp.maximum, ss), axis=-1, keepdims=True)
    m_new = jnp.maximum(m_prev, m_cur)
    alpha = jnp.exp(m_prev - m_new)
    acc = alpha[:, :1] * acc_s[...]
    psum = jnp.zeros(m_prev.shape, F32)
    for p in range(npg):
        pp = jnp.exp(ss[p] - m_new)
        psum = psum + pp
        acc = acc + _dot(pp.astype(BF16), ckv_refs[p][0].astype(BF16))
    m_s[...] = m_new
    l_new = alpha * l_s[...] + jnp.sum(psum, axis=-1, keepdims=True)
    l_s[...] = l_new
    acc_s[...] = acc

    @pl.when(c == pl.num_programs(1) - 1)
    def _():
        o_ref[0] = acc / l_new[:, :1]


def _sample_attn(page_flat, qabs, qpe, ckv_new, kpe_new, cache_ckv, cache_kpe, *, n_pages):
    db, h, kvl = qabs.shape
    page = cache_ckv.shape[1]
    npg = min(PAGES_PER_STEP, n_pages)
    assert n_pages % npg == 0
    nchunk = n_pages // npg
    scale = 1.0 / math.sqrt(D_QK)

    def page_map(k):
        return lambda b, c, pt: (pt[b * n_pages + c * npg + k], 0, 0)

    per_b = lambda b, c, pt: (b, 0, 0)
    in_specs = [
        pl.BlockSpec((1, h, kvl), per_b),
        pl.BlockSpec((1, h, LANES), per_b),
        pl.BlockSpec((1, 1, kvl), per_b),
        pl.BlockSpec((1, 1, LANES), per_b),
    ]
    in_specs += [pl.BlockSpec((1, page, kvl), page_map(k)) for k in range(npg)]
    in_specs += [pl.BlockSpec((1, page, D_ROPE), page_map(k)) for k in range(npg)]
    return pl.pallas_call(
        functools.partial(_sample_attn_kernel, npg=npg, scale=scale),
        grid_spec=pltpu.PrefetchScalarGridSpec(
            num_scalar_prefetch=1,
            grid=(db, nchunk),
            in_specs=in_specs,
            out_specs=pl.BlockSpec((1, h, kvl), per_b),
            scratch_shapes=[
                pltpu.VMEM((h, LANES), F32),
                pltpu.VMEM((h, LANES), F32),
                pltpu.VMEM((h, kvl), F32),
            ],
        ),
        out_shape=jax.ShapeDtypeStruct((db, h, kvl), F32),
        compiler_params=_cparams(("parallel", "arbitrary")),
        name="sample_attn",
    )(page_flat, qabs, qpe, ckv_new, kpe_new, *([cache_ckv] * npg), *([cache_kpe] * npg))


def _sample_value_kernel(o_ref, wuv_ref, att_ref, *, n_heads):
    for h in range(n_heads):
        att_ref[:, h * D_V:(h + 1) * D_V] = _dot(o_ref[h].astype(BF16), wuv_ref[h]).astype(BF16)


def _sample_value(o_lat_t, wuv3):
    h, db, kvl = o_lat_t.shape
    const3 = lambda i: (0, 0, 0)
    return pl.pallas_call(
        functools.partial(_sample_value_kernel, n_heads=h),
        grid=(1,),
        in_specs=[pl.BlockSpec((h, db, kvl), const3), pl.BlockSpec(wuv3.shape, const3)],
        out_specs=pl.BlockSpec((db, h * D_V), lambda i: (0, 0)),
        out_shape=jax.ShapeDtypeStruct((db, h * D_V), BF16),
        compiler_params=_cparams(("arbitrary",)),
        name="sample_value",
    )(o_lat_t, wuv3)


def _merge_proj_kernel(att_ref, yb_ref, sa_ref, sc_ref, wap_ref, wcp_ref, m_ref):
    a = _dot(att_ref[...], wap_ref[...])
    c = _dot(yb_ref[...], wcp_ref[...])
    m_ref[...] = (sa_ref[...] * a + sc_ref[...] * c).astype(BF16)


def _merge_proj(att, yb, sa, sc, wap, wcp):
    t, da = att.shape
    dc, d = wcp.shape
    tm = _pick(t, ROW_CAP, BF16_ROWS)
    tn = _pick(d, MERGE_COL_CAP, LANES)
    return pl.pallas_call(
        _merge_proj_kernel,
        grid=(d // tn, t // tm),
        in_specs=[
            pl.BlockSpec((tm, da), lambda j, i: (i, 0)),
            pl.BlockSpec((tm, dc), lambda j, i: (i, 0)),
            pl.BlockSpec((tm, tn), lambda j, i: (i, j)),
            pl.BlockSpec((tm, tn), lambda j, i: (i, j)),
            pl.BlockSpec((da, tn), lambda j, i: (0, j)),
            pl.BlockSpec((dc, tn), lambda j, i: (0, j)),
        ],
        out_specs=pl.BlockSpec((tm, tn), lambda j, i: (i, j)),
        out_shape=jax.ShapeDtypeStruct((t, d), BF16),
        compiler_params=_cparams(("parallel", "parallel")),
        name="merge_proj",
    )(att, yb, sa, sc, wap, wcp)


def _out_ln_router_kernel(x_ref, g0_ref, b0_ref, m_ref, wout_ref, g1_ref, b1_ref, rwh_ref, rwl_ref, rb_ref,
                          cin_ref, h1_ref, ri_ref, rw_ref, cnt_ref, carry_ref, *, alpha, n_experts):
    i = pl.program_id(0)

    @pl.when(i == 0)
    def _():
        carry_ref[...] = jnp.broadcast_to(cin_ref[...], carry_ref.shape)

    xn = _layer_norm(x_ref[...], g0_ref[...], b0_ref[...])
    h1 = _layer_norm(alpha * xn + _dot(m_ref[...], wout_ref[...]), g1_ref[...], b1_ref[...])
    h1_ref[...] = h1
    tm = h1.shape[0]
    hh = h1.astype(BF16)
    hl = (h1 - hh.astype(F32)).astype(BF16)
    rwh = rwh_ref[...]
    logits = _dot(hh, rwh) + _dot(hl, rwh) + _dot(hh, rwl_ref[...]) + rb_ref[...]
    lane = lax.broadcasted_iota(jnp.int32, (tm, LANES), 1)
    work = jnp.where(lane < n_experts, logits, -jnp.inf)
    vals, hots = [], []
    sel = jnp.zeros((tm, LANES), F32)
    ri = jnp.zeros((tm, LANES), jnp.int32)
    for k in range(TOP_K):
        mx = jnp.max(work, axis=-1, keepdims=True)
        idx = jnp.min(jnp.where(work == mx, lane, LANES), axis=-1, keepdims=True)
        hot = lane == idx
        work = jnp.where(hot, -jnp.inf, work)
        vals.append(mx)
        hots.append(hot)
        sel = sel + hot.astype(F32)
        ri = jnp.where(lane == k, idx, ri)
    es = [jnp.exp(v - vals[0]) for v in vals]
    den = es[0] + es[1] + es[2] + es[3]
    rw = jnp.zeros((tm, LANES), F32)
    for k in range(TOP_K):
        rw = jnp.where(lane == k, es[k] / den, rw)
    rw_ref[...] = rw
    r_i = lax.broadcasted_iota(jnp.int32, (tm, tm), 0)
    c_i = lax.broadcasted_iota(jnp.int32, (tm, tm), 1)
    tri = (c_i < r_i).astype(BF16)
    carry = carry_ref[0:1, :]
    rank_mat = _dot(tri, sel.astype(BF16)) + carry
    for k in range(TOP_K):
        rk = jnp.sum(jnp.where(hots[k], rank_mat, 0.0), axis=-1, keepdims=True).astype(jnp.int32)
        ri = jnp.where(lane == TOP_K + k, rk, ri)
    ri_ref[...] = ri
    new_carry = carry + jnp.sum(sel, axis=0, keepdims=True)
    carry_ref[0:1, :] = new_carry
    cnt_ref[...] = new_carry


def _out_ln_router(x, ln0_g, ln0_b, m, wout, ln1_g, ln1_b, rwh, rwl, rb, cnt_in, *, alpha, n_experts):
    t, d = x.shape
    tm = _pick(t, ROW_CAP, BF16_ROWS)
    row = lambda i: (i, 0)
    const = lambda i: (0, 0)
    return pl.pallas_call(
        functools.partial(_out_ln_router_kernel, alpha=alpha, n_experts=n_experts),
        grid=(t // tm,),
        in_specs=[
            pl.BlockSpec((tm, d), row),
            pl.BlockSpec((1, d), const),
            pl.BlockSpec((1, d), const),
            pl.BlockSpec((tm, d), row),
            pl.BlockSpec((d, d), const),
            pl.BlockSpec((1, d), const),
            pl.BlockSpec((1, d), const),
            pl.BlockSpec((d, LANES), const),
            pl.BlockSpec((d, LANES), const),
            pl.BlockSpec((1, LANES), const),
            pl.BlockSpec((1, LANES), const),
        ],
        out_specs=[
            pl.BlockSpec((tm, d), row),
            pl.BlockSpec((tm, LANES), row),
            pl.BlockSpec((tm, LANES), row),
            pl.BlockSpec((1, LANES), const),
        ],
        out_shape=[
            jax.ShapeDtypeStruct((t, d), F32),
            jax.ShapeDtypeStruct((t, LANES), jnp.int32),
            jax.ShapeDtypeStruct((t, LANES), F32),
            jax.ShapeDtypeStruct((1, LANES), F32),
        ],
        scratch_shapes=[pltpu.VMEM((SUBLANES, LANES), F32)],
        compiler_params=_cparams(("arbitrary",)),
        name="out_ln_router",
    )(x, ln0_g, ln0_b, m, wout, ln1_g, ln1_b, rwh, rwl, rb, cnt_in)


def _moe_scatter_kernel(offs_ref, e_ref, r_ref, h_ref, xs_in, xs_ref, sem):
    del xs_in
    tm = h_ref.shape[0]

    def issue(t, carry):
        for k in range(TOP_K):
            j = t * TOP_K + k
            dest = offs_ref[e_ref[j]] + r_ref[j]
            pltpu.make_async_copy(h_ref.at[pl.ds(t, 1)], xs_ref.at[pl.ds(dest, 1)], sem).start()
        return carry

    lax.fori_loop(0, tm, issue, 0)
    for _ in range(TOP_K):
        pltpu.make_async_copy(h_ref, xs_ref.at[pl.ds(0, tm)], sem).wait()


def _moe_scatter(offs, eidx, rank, h1, xs0):
    t, d = h1.shape
    tm = _pick(t, GATHER_ROWS, SUBLANES)
    return pl.pallas_call(
        _moe_scatter_kernel,
        grid_spec=pltpu.PrefetchScalarGridSpec(
            num_scalar_prefetch=1,
            grid=(t // tm,),
            in_specs=[
                pl.BlockSpec((tm * TOP_K,), lambda i, o: (i,), memory_space=pltpu.SMEM),
                pl.BlockSpec((tm * TOP_K,), lambda i, o: (i,), memory_space=pltpu.SMEM),
                pl.BlockSpec((tm, d), lambda i, o: (i, 0)),
                pl.BlockSpec(memory_space=pl.ANY),
            ],
            out_specs=pl.BlockSpec(memory_space=pl.ANY),
            scratch_shapes=[pltpu.SemaphoreType.DMA],
        ),
        out_shape=jax.ShapeDtypeStruct(xs0.shape, xs0.dtype),
        input_output_aliases={4: 0},
        compiler_params=_cparams(("arbitrary",)),
        name="moe_scatter",
    )(offs, eidx, rank, h1, xs0)


def _moe_up_kernel(te_ref, nu_ref, xs_ref, wg_ref, wu_ref, bg_ref, bu_ref, hm_ref):
    del te_ref
    j = pl.program_id(1)

    @pl.when(j < nu_ref[0])
    def _():
        x = xs_ref[...].astype(BF16)
        g = _dot(x, wg_ref[0]) + bg_ref[0]
        u = _dot(x, wu_ref[0]) + bu_ref[0]
        gate = jnp.minimum(g, SWIGLU_LIMIT)
        up = jnp.clip(u, -SWIGLU_LIMIT, SWIGLU_LIMIT)
        hm_ref[...] = ((up + 1.0) * (gate * jax.nn.sigmoid(SWIGLU_ALPHA * gate))).astype(BF16)

    @pl.when(j >= nu_ref[0])
    def _():
        hm_ref[...] = jnp.zeros(hm_ref.shape, BF16)


def _moe_up(tile_e, n_used, xs, wgu, bgu):
    s, d = xs.shape
    f = wgu.shape[2] // 2
    tf = _pick(f, FF_COL_CAP, LANES)
    nf = f // tf
    nt = s // EXPERT_ROWS
    return pl.pallas_call(
        _moe_up_kernel,
        grid_spec=pltpu.PrefetchScalarGridSpec(
            num_scalar_prefetch=2,
            grid=(nf, nt),
            in_specs=[
                pl.BlockSpec((EXPERT_ROWS, d), lambda n, j, te, nu: (jnp.minimum(j, nu[0] - 1), 0)),
                pl.BlockSpec((1, d, tf), lambda n, j, te, nu: (te[j], 0, n)),
                pl.BlockSpec((1, d, tf), lambda n, j, te, nu: (te[j], 0, nf + n)),
                pl.BlockSpec((1, 1, tf), lambda n, j, te, nu: (te[j], 0, n)),
                pl.BlockSpec((1, 1, tf), lambda n, j, te, nu: (te[j], 0, nf + n)),
            ],
            out_specs=pl.BlockSpec((EXPERT_ROWS, tf), lambda n, j, te, nu: (j, n)),
        ),
        out_shape=jax.ShapeDtypeStruct((s, f), BF16),
        compiler_params=_cparams(("parallel", "arbitrary")),
        name="moe_up",
    )(tile_e, n_used, xs, wgu, wgu, bgu, bgu)


def _moe_down_kernel(te_ref, nu_ref, hm_ref, wd_ref, bd_ref, y_ref):
    del te_ref
    j = pl.program_id(0)

    @pl.when(j < nu_ref[0])
    def _():
        y_ref[...] = _dot(hm_ref[...], wd_ref[0]) + bd_ref[0]

    @pl.when(j >= nu_ref[0])
    def _():
        y_ref[...] = jnp.zeros(y_ref.shape, F32)


def _moe_down(tile_e, n_used, hm, wd, bd):
    s, f = hm.shape
    d = wd.shape[2]
    nt = s // EXPERT_ROWS
    return pl.pallas_call(
        _moe_down_kernel,
        grid_spec=pltpu.PrefetchScalarGridSpec(
            num_scalar_prefetch=2,
            grid=(nt,),
            in_specs=[
                pl.BlockSpec((EXPERT_ROWS, f), lambda j, te, nu: (jnp.minimum(j, nu[0] - 1), 0)),
                pl.BlockSpec((1, f, d), lambda j, te, nu: (te[j], 0, 0)),
                pl.BlockSpec((1, 1, d), lambda j, te, nu: (te[j], 0, 0)),
            ],
            out_specs=pl.BlockSpec((EXPERT_ROWS, d), lambda j, te, nu: (j, 0)),
        ),
        out_shape=jax.ShapeDtypeStruct((s, d), F32),
        compiler_params=_cparams(("arbitrary",)),
        name="moe_down",
    )(tile_e, n_used, hm, wd, bd)


def _moe_combine_kernel(offs_ref, e_ref, r_ref, h1_ref, rw_ref, g2_ref, b2_ref, y_ref, out_ref, ybuf, sem,
                        *, alpha):
    tm = h1_ref.shape[0]

    def issue(t, carry):
        for k in range(TOP_K):
            j = t * TOP_K + k
            src = offs_ref[e_ref[j]] + r_ref[j]
            pltpu.make_async_copy(y_ref.at[pl.ds(src, 1)], ybuf.at[k, pl.ds(t, 1)], sem).start()
        return carry

    lax.fori_loop(0, tm, issue, 0)
    for k in range(TOP_K):
        pltpu.make_async_copy(y_ref.at[pl.ds(0, tm)], ybuf.at[k], sem).wait()
    rw = rw_ref[...]
    acc = alpha * h1_ref[...]
    for k in range(TOP_K):
        acc = acc + rw[:, k:k + 1] * ybuf[k]
    out_ref[...] = _layer_norm(acc, g2_ref[...], b2_ref[...])


def _moe_combine(offs, eidx, rank, h1, rw, ln2_g, ln2_b, y, *, alpha):
    t, d = h1.shape
    tm = _pick(t, GATHER_ROWS, SUBLANES)
    return pl.pallas_call(
        functools.partial(_moe_combine_kernel, alpha=alpha),
        grid_spec=pltpu.PrefetchScalarGridSpec(
            num_scalar_prefetch=1,
            grid=(t // tm,),
            in_specs=[
                pl.BlockSpec((tm * TOP_K,), lambda i, o: (i,), memory_space=pltpu.SMEM),
                pl.BlockSpec((tm * TOP_K,), lambda i, o: (i,), memory_space=pltpu.SMEM),
                pl.BlockSpec((tm, d), lambda i, o: (i, 0)),
                pl.BlockSpec((tm, LANES), lambda i, o: (i, 0)),
                pl.BlockSpec((1, d), lambda i, o: (0, 0)),
                pl.BlockSpec((1, d), lambda i, o: (0, 0)),
                pl.BlockSpec(memory_space=pl.ANY),
            ],
            out_specs=pl.BlockSpec((tm, d), lambda i, o: (i, 0)),
            scratch_shapes=[pltpu.VMEM((TOP_K, tm, d), F32), pltpu.SemaphoreType.DMA],
        ),
        out_shape=jax.ShapeDtypeStruct((t, d), F32),
        compiler_params=_cparams(("arbitrary",)),
        name="moe_combine",
    )(offs, eidx, rank, h1, rw, ln2_g, ln2_b, y)


def _rope_tables(pos):
    half = D_ROPE // 2
    inv_freq = ROPE_THETA ** (-jnp.arange(half, dtype=F32) / half)
    ang = pos.astype(F32)[:, None] * inv_freq[None, :]
    zeros = jnp.zeros((pos.shape[0], LANES - D_ROPE), F32)
    cos, sin = jnp.cos(ang), jnp.sin(ang)
    return jnp.concatenate([cos, cos, zeros], axis=1), jnp.concatenate([sin, sin, zeros], axis=1)


def _rot_cols(w):
    half = D_ROPE // 2
    return jnp.concatenate([-w[..., half:], w[..., :half]], axis=-1)


def _layer(d_idx, x_p, x_s, seq, past_len, cache_kv_latent, cache_k_rope, state_conv, page_flat, ln0_g, ln0_b,
           w_in, g_q_norm, w_q_up, g_kv_norm, w_kv_up, w_attn_proj, conv_w, w_conv_proj, w_out, ln1_g, ln1_b,
           router_w, router_b, w_gate_up, b_gate_up, w_down, b_down, ln2_g, ln2_b, *, alpha):
    tp, d = x_p.shape
    db = x_s.shape[0]
    ql, kvl = g_q_norm.shape[-1], g_kv_norm.shape[-1]
    h = N_HEADS
    e = router_w.shape[-1]
    page = cache_kv_latent.shape[2]
    n_pages = past_len // page

    w_in_d = w_in[d_idx]
    o = ql + kvl
    w_kpe = w_in_d[:, o:o + D_ROPE]
    zpad = jnp.zeros((d, LANES - D_ROPE), F32)
    wa = jnp.concatenate([w_in_d[:, :o], w_kpe, zpad, _rot_cols(w_kpe), zpad], axis=1).astype(BF16)
    dc = conv_w.shape[-1]
    assert w_in_d.shape[1] == o + D_ROPE + 3 * dc + 2 * d and dc == d
    w5 = w_in_d[:, o + D_ROPE:].reshape(d, 5, d).transpose(1, 0, 2).astype(BF16)
    wq = w_q_up[d_idx].reshape(ql, h, D_QK)
    zq = jnp.zeros((ql, h, LANES - D_ROPE), F32)
    wq3 = jnp.concatenate([wq, zq], axis=2).transpose(1, 0, 2).astype(BF16)
    wqr3 = jnp.concatenate([_rot_cols(wq[:, :, D_NOPE:]), zq], axis=2).transpose(1, 0, 2).astype(BF16)
    wkv = w_kv_up[d_idx].reshape(kvl, h, D_NOPE + D_V)
    wk3 = wkv[:, :, :D_NOPE].transpose(1, 0, 2).astype(BF16)
    wv3 = wkv[:, :, D_NOPE:].transpose(1, 0, 2).astype(BF16)
    wuk3 = wkv[:, :, :D_NOPE].transpose(1, 2, 0).astype(BF16)
    gq = g_q_norm[d_idx][None]
    gkv = g_kv_norm[d_idx][None]
    wap = w_attn_proj[d_idx].astype(BF16)
    wcp = w_conv_proj[d_idx].astype(BF16)
    wout = w_out[d_idx].astype(BF16)
    rw_pad = jnp.pad(router_w[d_idx], ((0, 0), (0, LANES - e)))
    rwh = rw_pad.astype(BF16)
    rwl = (rw_pad - rwh.astype(F32)).astype(BF16)
    rb = jnp.pad(router_b[d_idx], (0, LANES - e))[None]
    wgu = w_gate_up[d_idx].astype(BF16)
    bgu = b_gate_up[d_idx][:, None, :]
    wd = w_down[d_idx].astype(BF16)
    bd = b_down[d_idx][:, None, :]
    g1, b1 = ln1_g[d_idx][None], ln1_b[d_idx][None]
    g2, b2 = ln2_g[d_idx][None], ln2_b[d_idx][None]
    c_p, s_p = _rope_tables(jnp.arange(seq))
    c_s, s_s = _rope_tables(jnp.full((1,), past_len))
    cw = conv_w[d_idx]
    st = state_conv[d_idx]

    xn_p, ckv_p, kpe_p, qn_p, ckvb_p = _ln_in_proj(x_p, ln0_g, ln0_b, wa, gq, gkv, c_p, s_p, seq=seq)
    xn_s, ckv_s, kpe_s, qabs_t, qpe_t = _sample_in_proj(x_s, ln0_g, ln0_b, wa, gq, gkv, c_s, s_s, wq3, wqr3, wuk3)
    yb_p, sa_p, sc_p, conv_p = _conv_gate_prompt(xn_p, w5, cw, seq=seq)
    yb_s, sa_s, sc_s, u_s = _conv_gate_sample(xn_s, w5, cw, st[:, 0], st[:, 1])

    att_p = _prompt_attn(qn_p, ckvb_p, kpe_p, c_p, s_p, wq3, wqr3, wk3, wv3, seq=seq)
    o_lat = _sample_attn(page_flat, qabs_t.transpose(1, 0, 2), qpe_t.transpose(1, 0, 2),
                         ckv_s[:, None, :], kpe_s[:, None, :],
                         cache_kv_latent[d_idx], cache_k_rope[d_idx], n_pages=n_pages)
    att_s = _sample_value(o_lat.transpose(1, 0, 2), wv3)

    m_p = _merge_proj(att_p, yb_p, sa_p, sc_p, wap, wcp)
    m_s = _merge_proj(att_s, yb_s, sa_s, sc_s, wap, wcp)
    router = functools.partial(_out_ln_router, alpha=alpha, n_experts=e)
    h1_p, ri_p, rw_p, cnt_p = router(x_p, ln0_g, ln0_b, m_p, wout, g1, b1, rwh, rwl, rb, jnp.zeros((1, LANES), F32))
    h1_s, ri_s, rw_s, cnt_all = router(x_s, ln0_g, ln0_b, m_s, wout, g1, b1, rwh, rwl, rb, cnt_p)

    cnt = cnt_all[0, :e].astype(jnp.int32)
    tiles = (cnt + EXPERT_ROWS - 1) // EXPERT_ROWS
    tile_end = jnp.cumsum(tiles)
    offs = ((tile_end - tiles) * EXPERT_ROWS).astype(jnp.int32)
    n_tiles = -(-((tp + db) * TOP_K) // EXPERT_ROWS) + e
    n_used = tile_end[-1:].astype(jnp.int32)
    tile_e = jnp.sum(jnp.arange(n_tiles)[:, None] >= tile_end[None, :], axis=1)
    tile_e = jnp.minimum(tile_e, jnp.take(tile_e, n_used[0] - 1)).astype(jnp.int32)
    split = lambda ri: (ri[:, :TOP_K].reshape(-1), ri[:, TOP_K:2 * TOP_K].reshape(-1))
    e_p, r_p = split(ri_p)
    e_s, r_s = split(ri_s)
    xs = jnp.zeros((n_tiles * EXPERT_ROWS, d), F32)
    xs = _moe_scatter(offs, e_p, r_p, h1_p, xs)
    xs = _moe_scatter(offs, e_s, r_s, h1_s, xs)
    hm = _moe_up(tile_e, n_used, xs, wgu, bgu)
    y = _moe_down(tile_e, n_used, hm, wd, bd)
    out_p = _moe_combine(offs, e_p, r_p, h1_p, rw_p, g2, b2, y, alpha=alpha)
    out_s = _moe_combine(offs, e_s, r_s, h1_s, rw_s, g2, b2, y, alpha=alpha)

    conv_s = jnp.stack([st[:, 1], u_s], axis=1)
    return out_p, out_s, ckv_p, kpe_p[:, :D_ROPE], conv_p, ckv_s, kpe_s[:, :D_ROPE], conv_s


def kernel(x_prompt, x_sample, cache_kv_latent, cache_k_rope, state_conv, page_table, meta_tokens, ln0_g, ln0_b,
           w_in, g_q_norm, w_q_up, g_kv_norm, w_kv_up, w_attn_proj, conv_w, w_conv_proj, w_out, ln1_g, ln1_b,
           router_w, router_b, w_gate_up, b_gate_up, w_down, b_down, ln2_g, ln2_b):
    nb, s_len, d = x_prompt.shape
    db, dec_seq, _ = x_sample.shape
    depth = w_in.shape[0]
    assert dec_seq == 1 and depth == 1
    n_meta = meta_tokens.shape[0]
    seq = s_len + n_meta
    tp = nb * seq
    past_len = page_table.shape[1] * cache_kv_latent.shape[2]
    alpha = (2 * depth) ** 0.25
    meta = jnp.broadcast_to(meta_tokens[None].astype(x_prompt.dtype), (nb, n_meta, d))
    x_p = jnp.concatenate([meta, x_prompt], axis=1).reshape(tp, d)
    x_s = x_sample.reshape(db, d)
    out_p, out_s, ckv_p, kpe_p, conv_p, ckv_s, kpe_s, conv_s = _layer(
        0, x_p, x_s, seq, past_len, cache_kv_latent, cache_k_rope, state_conv, page_table.reshape(-1),
        ln0_g[None], ln0_b[None], w_in, g_q_norm, w_q_up, g_kv_norm, w_kv_up, w_attn_proj, conv_w,
        w_conv_proj, w_out, ln1_g, ln1_b, router_w, router_b, w_gate_up, b_gate_up, w_down, b_down,
        ln2_g, ln2_b, alpha=alpha)
    kvl = ckv_p.shape[-1]
    return (out_p.reshape(nb, seq, d)[:, n_meta:], out_s.reshape(db, 1, d),
            ckv_p.reshape(1, nb, seq, kvl), kpe_p.reshape(1, nb, seq, D_ROPE), conv_p[None],
            ckv_s.reshape(1, db, 1, kvl), kpe_s.reshape(1, db, 1, D_ROPE), conv_s[None])
```

```python
import functools
import math

import jax
import jax.numpy as jnp
from jax import lax
from jax.experimental import pallas as pl
from jax.experimental.pallas import tpu as pltpu

F32 = jnp.float32
BF16 = jnp.bfloat16

N_HEADS = 16
D_NOPE = 128
D_ROPE = 64
D_V = 128
D_QK = D_NOPE + D_ROPE
ROPE_THETA = 10000.0
CONV_W = 3
TOP_K = 4
SWIGLU_LIMIT = 7.0
SWIGLU_ALPHA = 1.702
LN_EPS = 1e-5
RMS_EPS = 1e-6
NEG_INF = -1e30

LANES = 128
SUBLANES = 8
BF16_ROWS = 16
VMEM_LIMIT = 56 * 1024 * 1024

ROW_CAP = 704
CONV_COL_CAP = 512
MERGE_COL_CAP = 1024
ATTN_Q_ROWS = 512
PAGES_PER_STEP = 16
EXPERT_ROWS = 256
FF_COL_CAP = 512
DOWN_COL_CAP = 1024
GATHER_ROWS = 128


def _pick(n, cap, mult):
    best = None
    for c in range(mult, min(n, cap) + 1, mult):
        if n % c == 0:
            best = c
    if best is None:
        raise ValueError(f"no block of {n} that is a multiple of {mult} and <= {cap}")
    return best


def _cparams(sem):
    return pltpu.CompilerParams(dimension_semantics=sem, vmem_limit_bytes=VMEM_LIMIT)


def _layer_norm(x, g, b):
    mu = jnp.mean(x, axis=-1, keepdims=True)
    xc = x - mu
    var = jnp.mean(xc * xc, axis=-1, keepdims=True)
    return xc * lax.rsqrt(var + LN_EPS) * g + b


def _rms_norm(x, g):
    return x * lax.rsqrt(jnp.mean(x * x, axis=-1, keepdims=True) + RMS_EPS) * g


def _dot(a, b):
    return jnp.dot(a, b, preferred_element_type=F32)


def _dot_nt(a, b):
    return lax.dot_general(a, b, (((1,), (1,)), ((), ())), preferred_element_type=F32)


def _ln_in_proj_kernel(x_ref, g0_ref, b0_ref, wa_ref, gq_ref, gkv_ref, c_ref, s_ref,
                       xn_ref, ckv_ref, kpe_ref, qn_ref, ckvb_ref, *, ql, kvl):
    xn = _layer_norm(x_ref[...], g0_ref[...], b0_ref[...]).astype(BF16)
    xn_ref[...] = xn
    za = _dot(xn, wa_ref[...])
    ckv = _rms_norm(za[:, ql:ql + kvl], gkv_ref[...])
    ckv_ref[...] = ckv
    ckvb_ref[...] = ckv.astype(BF16)
    o = ql + kvl
    kpe_ref[...] = za[:, o:o + LANES] * c_ref[...] + za[:, o + LANES:o + 2 * LANES] * s_ref[...]
    qn_ref[...] = _rms_norm(za[:, :ql], gq_ref[...]).astype(BF16)


def _ln_in_proj(x, ln0_g, ln0_b, wa, gq, gkv, c128, s128, *, seq):
    tp, d = x.shape
    ql, kvl = gq.shape[1], gkv.shape[1]
    tm = _pick(seq, ROW_CAP, BF16_ROWS)
    nsb = seq // tm
    row = lambda i: (i, 0)
    const = lambda i: (0, 0)
    return pl.pallas_call(
        functools.partial(_ln_in_proj_kernel, ql=ql, kvl=kvl),
        grid=(tp // tm,),
        in_specs=[
            pl.BlockSpec((tm, d), row),
            pl.BlockSpec((1, d), const),
            pl.BlockSpec((1, d), const),
            pl.BlockSpec(wa.shape, const),
            pl.BlockSpec((1, ql), const),
            pl.BlockSpec((1, kvl), const),
            pl.BlockSpec((tm, LANES), lambda i: (i % nsb, 0)),
            pl.BlockSpec((tm, LANES), lambda i: (i % nsb, 0)),
        ],
        out_specs=[
            pl.BlockSpec((tm, d), row),
            pl.BlockSpec((tm, kvl), row),
            pl.BlockSpec((tm, LANES), row),
            pl.BlockSpec((tm, ql), row),
            pl.BlockSpec((tm, kvl), row),
        ],
        out_shape=[
            jax.ShapeDtypeStruct((tp, d), BF16),
            jax.ShapeDtypeStruct((tp, kvl), F32),
            jax.ShapeDtypeStruct((tp, LANES), F32),
            jax.ShapeDtypeStruct((tp, ql), BF16),
            jax.ShapeDtypeStruct((tp, kvl), BF16),
        ],
        compiler_params=_cparams(("parallel",)),
        name="ln_in_proj",
    )(x, ln0_g, ln0_b, wa, gq, gkv, c128, s128)


def _sample_in_proj_kernel(x_ref, g0_ref, b0_ref, wa_ref, gq_ref, gkv_ref, c_ref, s_ref,
                           wq_ref, wqr_ref, wuk_ref,
                           xn_ref, ckv_ref, kpe_ref, qabs_ref, qpe_ref, *, ql, kvl, n_heads):
    xn = _layer_norm(x_ref[...], g0_ref[...], b0_ref[...]).astype(BF16)
    xn_ref[...] = xn
    za = _dot(xn, wa_ref[...])
    ckv_ref[...] = _rms_norm(za[:, ql:ql + kvl], gkv_ref[...])
    o = ql + kvl
    c = c_ref[...]
    s = s_ref[...]
    kpe_ref[...] = za[:, o:o + LANES] * c + za[:, o + LANES:o + 2 * LANES] * s
    qn = _rms_norm(za[:, :ql], gq_ref[...]).astype(BF16)
    for h in range(n_heads):
        qa = _dot(qn, wq_ref[h])
        qb = _dot(qn, wqr_ref[h])
        qpe_ref[h] = (qa[:, D_NOPE:] * c + qb * s).astype(BF16)
        qabs_ref[h] = _dot(qa[:, :D_NOPE].astype(BF16), wuk_ref[h]).astype(BF16)


def _sample_in_proj(x, ln0_g, ln0_b, wa, gq, gkv, c128, s128, wq3, wqr3, wuk3):
    db, d = x.shape
    ql, kvl = gq.shape[1], gkv.shape[1]
    h = wq3.shape[0]
    const2 = lambda i: (0, 0)
    const3 = lambda i: (0, 0, 0)
    return pl.pallas_call(
        functools.partial(_sample_in_proj_kernel, ql=ql, kvl=kvl, n_heads=h),
        grid=(1,),
        in_specs=[
            pl.BlockSpec((db, d), const2),
            pl.BlockSpec((1, d), const2),
            pl.BlockSpec((1, d), const2),
            pl.BlockSpec(wa.shape, const2),
            pl.BlockSpec((1, ql), const2),
            pl.BlockSpec((1, kvl), const2),
            pl.BlockSpec((1, LANES), const2),
            pl.BlockSpec((1, LANES), const2),
            pl.BlockSpec(wq3.shape, const3),
            pl.BlockSpec(wqr3.shape, const3),
            pl.BlockSpec(wuk3.shape, const3),
        ],
        out_specs=[
            pl.BlockSpec((db, d), const2),
            pl.BlockSpec((db, kvl), const2),
            pl.BlockSpec((db, LANES), const2),
            pl.BlockSpec((h, db, kvl), const3),
            pl.BlockSpec((h, db, LANES), const3),
        ],
        out_shape=[
            jax.ShapeDtypeStruct((db, d), BF16),
            jax.ShapeDtypeStruct((db, kvl), F32),
            jax.ShapeDtypeStruct((db, LANES), F32),
            jax.ShapeDtypeStruct((h, db, kvl), BF16),
            jax.ShapeDtypeStruct((h, db, LANES), BF16),
        ],
        compiler_params=_cparams(("arbitrary",)),
        name="sample_in_proj",
    )(x, ln0_g, ln0_b, wa, gq, gkv, c128, s128, wq3, wqr3, wuk3)


def _conv_gate_prompt_kernel(xn_ref, w_ref, cw_ref, yb_ref, sa_ref, sc_ref, cst_ref, halo_ref, *, nsb):
    i = pl.program_id(1)
    xn = xn_ref[...]
    tm = xn.shape[0]
    u = _dot(xn, w_ref[1]) * _dot(xn, w_ref[2])

    @pl.when((i % nsb) == 0)
    def _():
        halo_ref[...] = jnp.zeros(halo_ref.shape, F32)

    prev = halo_ref[...]
    p1 = prev[SUBLANES - 1:SUBLANES]
    p2 = prev[SUBLANES - 2:SUBLANES - 1]
    row = lax.broadcasted_iota(jnp.int32, (tm, 1), 0)
    u1 = jnp.where(row == 0, p1, pltpu.roll(u, 1, 0))
    u2 = jnp.where(row == 0, p2, jnp.where(row == 1, p1, pltpu.roll(u, 2, 0)))
    cw = cw_ref[...]
    y = cw[0:1] * u2 + cw[1:2] * u1 + cw[2:3] * u
    halo_ref[...] = u[tm - SUBLANES:]
    cst_ref[0] = u[tm - (CONV_W - 1):]
    yb_ref[...] = (_dot(xn, w_ref[0]) * y).astype(BF16)
    sa_ref[...] = jax.nn.sigmoid(_dot(xn, w_ref[3]))
    sc_ref[...] = jax.nn.sigmoid(_dot(xn, w_ref[4]))


def _conv_gate_prompt(xn, w5, conv_w, *, seq):
    tp, d = xn.shape
    dc = w5.shape[2]
    tm = _pick(seq, ROW_CAP, BF16_ROWS)
    tn = _pick(dc, CONV_COL_CAP, LANES)
    nsb = seq // tm
    nb = tp // seq
    blk = lambda j, i: (i, j)
    return pl.pallas_call(
        functools.partial(_conv_gate_prompt_kernel, nsb=nsb),
        grid=(dc // tn, tp // tm),
        in_specs=[
            pl.BlockSpec((tm, d), lambda j, i: (i, 0)),
            pl.BlockSpec((5, d, tn), lambda j, i: (0, 0, j)),
            pl.BlockSpec((CONV_W, tn), lambda j, i: (0, j)),
        ],
        out_specs=[
            pl.BlockSpec((tm, tn), blk),
            pl.BlockSpec((tm, tn), blk),
            pl.BlockSpec((tm, tn), blk),
            pl.BlockSpec((1, CONV_W - 1, tn), lambda j, i: (i // nsb, 0, j)),
        ],
        out_shape=[
            jax.ShapeDtypeStruct((tp, dc), BF16),
            jax.ShapeDtypeStruct((tp, dc), F32),
            jax.ShapeDtypeStruct((tp, dc), F32),
            jax.ShapeDtypeStruct((nb, CONV_W - 1, dc), F32),
        ],
        scratch_shapes=[pltpu.VMEM((SUBLANES, tn), F32)],
        compiler_params=_cparams(("parallel", "arbitrary")),
        name="conv_gate_prompt",
    )(xn, w5, conv_w)


def _conv_gate_sample_kernel(xn_ref, w_ref, cw_ref, s0_ref, s1_ref, yb_ref, sa_ref, sc_ref, u_ref):
    xn = xn_ref[...]
    u = _dot(xn, w_ref[1]) * _dot(xn, w_ref[2])
    cw = cw_ref[...]
    y = cw[0:1] * s0_ref[...] + cw[1:2] * s1_ref[...] + cw[2:3] * u
    u_ref[...] = u
    yb_ref[...] = (_dot(xn, w_ref[0]) * y).astype(BF16)
    sa_ref[...] = jax.nn.sigmoid(_dot(xn, w_ref[3]))
    sc_ref[...] = jax.nn.sigmoid(_dot(xn, w_ref[4]))


def _conv_gate_sample(xn, w5, conv_w, st0, st1):
    db, d = xn.shape
    dc = w5.shape[2]
    tn = _pick(dc, CONV_COL_CAP, LANES)
    col = lambda j: (0, j)
    return pl.pallas_call(
        _conv_gate_sample_kernel,
        grid=(dc // tn,),
        in_specs=[
            pl.BlockSpec((db, d), lambda j: (0, 0)),
            pl.BlockSpec((5, d, tn), lambda j: (0, 0, j)),
            pl.BlockSpec((CONV_W, tn), col),
            pl.BlockSpec((db, tn), col),
            pl.BlockSpec((db, tn), col),
        ],
        out_specs=[pl.BlockSpec((db, tn), col)] * 4,
        out_shape=[
            jax.ShapeDtypeStruct((db, dc), BF16),
            jax.ShapeDtypeStruct((db, dc), F32),
            jax.ShapeDtypeStruct((db, dc), F32),
            jax.ShapeDtypeStruct((db, dc), F32),
        ],
        compiler_params=_cparams(("parallel",)),
        name="conv_gate_sample",
    )(xn, w5, conv_w, st0, st1)


def _prompt_attn_kernel(qn_ref, ckvb_ref, kpe_ref, c_ref, s_ref, wq_ref, wqr_ref, wk_ref, wv_ref,
                        att_ref, q_s, k_s, v_s, *, tq, scale):
    seq = qn_ref.shape[0]
    qn = qn_ref[...]
    qa = _dot(qn, wq_ref[0])
    qb = _dot(qn, wqr_ref[0])
    q_s[:, :D_NOPE] = qa[:, :D_NOPE].astype(BF16)
    q_s[:, D_NOPE:] = (qa[:, D_NOPE:] * c_ref[...] + qb * s_ref[...]).astype(BF16)
    ckvb = ckvb_ref[...]
    k_s[:, :D_NOPE] = _dot(ckvb, wk_ref[0]).astype(BF16)
    k_s[:, D_NOPE:] = kpe_ref[...].astype(BF16)
    v_s[...] = _dot(ckvb, wv_ref[0]).astype(BF16)
    for q0 in range(0, seq, tq):
        q1 = min(q0 + tq, seq)
        s = _dot_nt(q_s[q0:q1, :], k_s[:q1, :]) * scale
        qpos = q0 + lax.broadcasted_iota(jnp.int32, (q1 - q0, 1), 0)
        kpos = lax.broadcasted_iota(jnp.int32, (1, q1), 1)
        s = jnp.where(kpos <= qpos, s, NEG_INF)
        p = jnp.exp(s - jnp.max(s, axis=-1, keepdims=True))
        l = jnp.sum(p, axis=-1, keepdims=True)
        o = _dot(p.astype(BF16), v_s[:q1, :])
        att_ref[q0:q1, :] = (o / l).astype(BF16)


def _prompt_attn(qn, ckvb, kpe128, c128, s128, wq3, wqr3, wk3, wv3, *, seq):
    tp, ql = qn.shape
    kvl = ckvb.shape[1]
    h = wq3.shape[0]
    nb = tp // seq
    scale = 1.0 / math.sqrt(D_QK)
    per_b = lambda b, hh: (b, 0)
    const = lambda b, hh: (0, 0)
    per_h = lambda b, hh: (hh, 0, 0)
    return pl.pallas_call(
        functools.partial(_prompt_attn_kernel, tq=ATTN_Q_ROWS, scale=scale),
        grid=(nb, h),
        in_specs=[
            pl.BlockSpec((seq, ql), per_b),
            pl.BlockSpec((seq, kvl), per_b),
            pl.BlockSpec((seq, LANES), per_b),
            pl.BlockSpec((seq, LANES), const),
            pl.BlockSpec((seq, LANES), const),
            pl.BlockSpec((1, ql, 2 * LANES), per_h),
            pl.BlockSpec((1, ql, LANES), per_h),
            pl.BlockSpec((1, kvl, D_NOPE), per_h),
            pl.BlockSpec((1, kvl, D_V), per_h),
        ],
        out_specs=pl.BlockSpec((seq, D_V), lambda b, hh: (b, hh)),
        out_shape=jax.ShapeDtypeStruct((tp, h * D_V), BF16),
        scratch_shapes=[
            pltpu.VMEM((seq, 2 * LANES), BF16),
            pltpu.VMEM((seq, 2 * LANES), BF16),
            pltpu.VMEM((seq, D_V), BF16),
        ],
        compiler_params=_cparams(("parallel", "arbitrary")),
        name="prompt_attn",
    )(qn, ckvb, kpe128, c128, s128, wq3, wqr3, wk3, wv3)


def _sample_attn_kernel(pt_ref, qa_ref, qp_ref, cn_ref, kn_ref, ckv_hbm, kpe_hbm, o_ref,
                        ckv_buf, kpe_buf, sems, m_s, l_s, acc_s, *, npg, scale):
    c = pl.program_id(1)
    nchunk = pl.num_programs(1)
    g = pl.program_id(0) * nchunk + c
    total = pl.num_programs(0) * nchunk
    slot = g % 2

    def fetch(step, dst_slot):
        for k in range(npg):
            pg = pt_ref[step * npg + k]
            pltpu.make_async_copy(ckv_hbm.at[pg], ckv_buf.at[dst_slot, k], sems.at[dst_slot, 0]).start()
            pltpu.make_async_copy(kpe_hbm.at[pg], kpe_buf.at[dst_slot, k], sems.at[dst_slot, 1]).start()

    @pl.when(g == 0)
    def _():
        fetch(0, 0)

    @pl.when(g + 1 < total)
    def _():
        fetch(g + 1, 1 - slot)

    pltpu.make_async_copy(ckv_hbm.at[pl.ds(0, npg)], ckv_buf.at[slot], sems.at[slot, 0]).wait()
    pltpu.make_async_copy(kpe_hbm.at[pl.ds(0, npg)], kpe_buf.at[slot], sems.at[slot, 1]).wait()

    qa = qa_ref[0]
    qp = qp_ref[0][:, :D_ROPE]

    @pl.when(c == 0)
    def _():
        cn = cn_ref[0].astype(BF16).astype(F32)
        kn = kn_ref[0][:, :D_ROPE].astype(BF16).astype(F32)
        s0 = (jnp.sum(qa.astype(F32) * cn, axis=-1, keepdims=True)
              + jnp.sum(qp.astype(F32) * kn, axis=-1, keepdims=True)) * scale
        m_s[...] = jnp.broadcast_to(s0, m_s.shape)
        l_s[...] = jnp.ones(l_s.shape, F32)
        acc_s[...] = jnp.broadcast_to(cn, acc_s.shape)

    ss = []
    for p in range(npg):
        ck = ckv_buf[slot, p].astype(BF16)
        kp = kpe_buf[slot, p].astype(BF16)
        ss.append((_dot_nt(qa, ck) + _dot(qp, kp)) * scale)
    m_prev = m_s[...]
    m_cur = jnp.max(functools.reduce(jnp.maximum, ss), axis=-1, keepdims=True)
    m_new = jnp.maximum(m_prev, m_cur)
    alpha = jnp.exp(m_prev - m_new)
    acc = alpha[:, :1] * acc_s[...]
    psum = jnp.zeros(m_prev.shape, F32)
    for p in range(npg):
        pp = jnp.exp(ss[p] - m_new)
        psum = psum + pp
        acc = acc + _dot(pp.astype(BF16), ckv_buf[slot, p].astype(BF16))
    m_s[...] = m_new
    l_new = alpha * l_s[...] + jnp.sum(psum, axis=-1, keepdims=True)
    l_s[...] = l_new
    acc_s[...] = acc

    @pl.when(c == pl.num_programs(1) - 1)
    def _():
        o_ref[0] = acc / l_new[:, :1]


def _sample_attn(page_flat, qabs, qpe, ckv_new, kpe_new, cache_ckv, cache_kpe_t, *, n_pages):
    db, h, kvl = qabs.shape
    page = cache_ckv.shape[1]
    npg = min(PAGES_PER_STEP, n_pages)
    assert n_pages % npg == 0
    nchunk = n_pages // npg
    scale = 1.0 / math.sqrt(D_QK)
    per_b = lambda b, c, pt: (b, 0, 0)
    return pl.pallas_call(
        functools.partial(_sample_attn_kernel, npg=npg, scale=scale),
        grid_spec=pltpu.PrefetchScalarGridSpec(
            num_scalar_prefetch=1,
            grid=(db, nchunk),
            in_specs=[
                pl.BlockSpec((1, h, kvl), per_b),
                pl.BlockSpec((1, h, LANES), per_b),
                pl.BlockSpec((1, 1, kvl), per_b),
                pl.BlockSpec((1, 1, LANES), per_b),
                pl.BlockSpec(memory_space=pl.ANY),
                pl.BlockSpec(memory_space=pl.ANY),
            ],
            out_specs=pl.BlockSpec((1, h, kvl), per_b),
            scratch_shapes=[
                pltpu.VMEM((2, npg, page, kvl), F32),
                pltpu.VMEM((2, npg, D_ROPE, page), F32),
                pltpu.SemaphoreType.DMA((2, 2)),
                pltpu.VMEM((h, LANES), F32),
                pltpu.VMEM((h, LANES), F32),
                pltpu.VMEM((h, kvl), F32),
            ],
        ),
        out_shape=jax.ShapeDtypeStruct((db, h, kvl), F32),
        compiler_params=_cparams(("arbitrary", "arbitrary")),
        name="sample_attn",
    )(page_flat, qabs, qpe, ckv_new, kpe_new, cache_ckv, cache_kpe_t)


def _sample_value_kernel(o_ref, wuv_ref, att_ref, *, n_heads):
    for h in range(n_heads):
        att_ref[:, h * D_V:(h + 1) * D_V] = _dot(o_ref[h].astype(BF16), wuv_ref[h]).astype(BF16)


def _sample_value(o_lat_t, wuv3):
    h, db, kvl = o_lat_t.shape
    const3 = lambda i: (0, 0, 0)
    return pl.pallas_call(
        functools.partial(_sample_value_kernel, n_heads=h),
        grid=(1,),
        in_specs=[pl.BlockSpec((h, db, kvl), const3), pl.BlockSpec(wuv3.shape, const3)],
        out_specs=pl.BlockSpec((db, h * D_V), lambda i: (0, 0)),
        out_shape=jax.ShapeDtypeStruct((db, h * D_V), BF16),
        compiler_params=_cparams(("arbitrary",)),
        name="sample_value",
    )(o_lat_t, wuv3)


def _merge_proj_kernel(att_ref, yb_ref, sa_ref, sc_ref, wap_ref, wcp_ref, m_ref):
    a = _dot(att_ref[...], wap_ref[...])
    c = _dot(yb_ref[...], wcp_ref[...])
    m_ref[...] = (sa_ref[...] * a + sc_ref[...] * c).astype(BF16)


def _merge_proj(att, yb, sa, sc, wap, wcp):
    t, da = att.shape
    dc, d = wcp.shape
    tm = _pick(t, ROW_CAP, BF16_ROWS)
    tn = _pick(d, MERGE_COL_CAP, LANES)
    return pl.pallas_call(
        _merge_proj_kernel,
        grid=(d // tn, t // tm),
        in_specs=[
            pl.BlockSpec((tm, da), lambda j, i: (i, 0)),
            pl.BlockSpec((tm, dc), lambda j, i: (i, 0)),
            pl.BlockSpec((tm, tn), lambda j, i: (i, j)),
            pl.BlockSpec((tm, tn), lambda j, i: (i, j)),
            pl.BlockSpec((da, tn), lambda j, i: (0, j)),
            pl.BlockSpec((dc, tn), lambda j, i: (0, j)),
        ],
        out_specs=pl.BlockSpec((tm, tn), lambda j, i: (i, j)),
        out_shape=jax.ShapeDtypeStruct((t, d), BF16),
        compiler_params=_cparams(("parallel", "parallel")),
        name="merge_proj",
    )(att, yb, sa, sc, wap, wcp)


def _out_ln_router_kernel(x_ref, g0_ref, b0_ref, m_ref, wout_ref, g1_ref, b1_ref, rwh_ref, rwl_ref, rb_ref,
                          cin_ref, h1_ref, hp_ref, ri_ref, rw_ref, cnt_ref, carry_ref, *, alpha, n_experts):
    i = pl.program_id(0)

    @pl.when(i == 0)
    def _():
        carry_ref[...] = jnp.broadcast_to(cin_ref[...], carry_ref.shape)

    xn = _layer_norm(x_ref[...], g0_ref[...], b0_ref[...])
    h1 = _layer_norm(alpha * xn + _dot(m_ref[...], wout_ref[...]), g1_ref[...], b1_ref[...])
    h1_ref[...] = h1
    tm, d = h1.shape
    hh = h1.astype(BF16)
    hh32 = hh.astype(F32)
    bits = lax.bitcast_convert_type(hh32, jnp.uint32)
    hp_ref[...] = (bits[:, d // 2:] & jnp.uint32(0xFFFF0000)) | (bits[:, :d // 2] >> 16)
    hl = (h1 - hh32).astype(BF16)
    rwh = rwh_ref[...]
    logits = _dot(hh, rwh) + _dot(hl, rwh) + _dot(hh, rwl_ref[...]) + rb_ref[...]
    lane = lax.broadcasted_iota(jnp.int32, (tm, LANES), 1)
    work = jnp.where(lane < n_experts, logits, -jnp.inf)
    vals, hots = [], []
    sel = jnp.zeros((tm, LANES), F32)
    ri = jnp.zeros((tm, LANES), jnp.int32)
    for k in range(TOP_K):
        mx = jnp.max(work, axis=-1, keepdims=True)
        idx = jnp.min(jnp.where(work == mx, lane, LANES), axis=-1, keepdims=True)
        hot = lane == idx
        work = jnp.where(hot, -jnp.inf, work)
        vals.append(mx)
        hots.append(hot)
        sel = sel + hot.astype(F32)
        ri = jnp.where(lane == k, idx, ri)
    es = [jnp.exp(v - vals[0]) for v in vals]
    den = es[0] + es[1] + es[2] + es[3]
    rw = jnp.zeros((tm, LANES), F32)
    for k in range(TOP_K):
        rw = jnp.where(lane == k, es[k] / den, rw)
    rw_ref[...] = rw
    r_i = lax.broadcasted_iota(jnp.int32, (tm, tm), 0)
    c_i = lax.broadcasted_iota(jnp.int32, (tm, tm), 1)
    tri = (c_i < r_i).astype(BF16)
    carry = carry_ref[0:1, :]
    rank_mat = _dot(tri, sel.astype(BF16)) + carry
    for k in range(TOP_K):
        rk = jnp.sum(jnp.where(hots[k], rank_mat, 0.0), axis=-1, keepdims=True).astype(jnp.int32)
        ri = jnp.where(lane == TOP_K + k, rk, ri)
    ri_ref[...] = ri
    new_carry = carry + jnp.sum(sel, axis=0, keepdims=True)
    carry_ref[0:1, :] = new_carry
    cnt_ref[...] = new_carry


def _out_ln_router(x, ln0_g, ln0_b, m, wout, ln1_g, ln1_b, rwh, rwl, rb, cnt_in, *, alpha, n_experts):
    t, d = x.shape
    tm = _pick(t, ROW_CAP, BF16_ROWS)
    row = lambda i: (i, 0)
    const = lambda i: (0, 0)
    return pl.pallas_call(
        functools.partial(_out_ln_router_kernel, alpha=alpha, n_experts=n_experts),
        grid=(t // tm,),
        in_specs=[
            pl.BlockSpec((tm, d), row),
            pl.BlockSpec((1, d), const),
            pl.BlockSpec((1, d), const),
            pl.BlockSpec((tm, d), row),
            pl.BlockSpec((d, d), const),
            pl.BlockSpec((1, d), const),
            pl.BlockSpec((1, d), const),
            pl.BlockSpec((d, LANES), const),
            pl.BlockSpec((d, LANES), const),
            pl.BlockSpec((1, LANES), const),
            pl.BlockSpec((1, LANES), const),
        ],
        out_specs=[
            pl.BlockSpec((tm, d), row),
            pl.BlockSpec((tm, d // 2), row),
            pl.BlockSpec((tm, LANES), row),
            pl.BlockSpec((tm, LANES), row),
            pl.BlockSpec((1, LANES), const),
        ],
        out_shape=[
            jax.ShapeDtypeStruct((t, d), F32),
            jax.ShapeDtypeStruct((t, d // 2), jnp.uint32),
            jax.ShapeDtypeStruct((t, LANES), jnp.int32),
            jax.ShapeDtypeStruct((t, LANES), F32),
            jax.ShapeDtypeStruct((1, LANES), F32),
        ],
        scratch_shapes=[pltpu.VMEM((SUBLANES, LANES), F32)],
        compiler_params=_cparams(("arbitrary",)),
        name="out_ln_router",
    )(x, ln0_g, ln0_b, m, wout, ln1_g, ln1_b, rwh, rwl, rb, cnt_in)


def _moe_scatter_kernel(offs_ref, e_ref, r_ref, h_ref, xs_in, xs_ref, sem):
    del xs_in
    tm = h_ref.shape[0]

    def issue(t, carry):
        for k in range(TOP_K):
            j = t * TOP_K + k
            dest = offs_ref[e_ref[j]] + r_ref[j]
            pltpu.make_async_copy(h_ref.at[pl.ds(t, 1)], xs_ref.at[pl.ds(dest, 1)], sem).start()
        return carry

    lax.fori_loop(0, tm, issue, 0)
    for _ in range(TOP_K):
        pltpu.make_async_copy(h_ref, xs_ref.at[pl.ds(0, tm)], sem).wait()


def _moe_scatter(offs, eidx, rank, h1, xs0):
    t, d = h1.shape
    tm = _pick(t, GATHER_ROWS, SUBLANES)
    return pl.pallas_call(
        _moe_scatter_kernel,
        grid_spec=pltpu.PrefetchScalarGridSpec(
            num_scalar_prefetch=1,
            grid=(t // tm,),
            in_specs=[
                pl.BlockSpec((tm * TOP_K,), lambda i, o: (i,), memory_space=pltpu.SMEM),
                pl.BlockSpec((tm * TOP_K,), lambda i, o: (i,), memory_space=pltpu.SMEM),
                pl.BlockSpec((tm, d), lambda i, o: (i, 0)),
                pl.BlockSpec(memory_space=pl.ANY),
            ],
            out_specs=pl.BlockSpec(memory_space=pl.ANY),
            scratch_shapes=[pltpu.SemaphoreType.DMA],
        ),
        out_shape=jax.ShapeDtypeStruct(xs0.shape, xs0.dtype),
        input_output_aliases={4: 0},
        compiler_params=_cparams(("arbitrary",)),
        name="moe_scatter",
    )(offs, eidx, rank, h1, xs0)


def _expert_changed(te_ref, j):
    return (j == 0) | (te_ref[j] != te_ref[jnp.maximum(j - 1, 0)])


def _moe_up_kernel(te_ref, nu_ref, xs_ref, wg_ref, wu_ref, bg_ref, bu_ref, hm_ref, wg_s, wu_s):
    j = pl.program_id(1)

    @pl.when(_expert_changed(te_ref, j))
    def _():
        wg_s[...] = wg_ref[0].astype(BF16)
        wu_s[...] = wu_ref[0].astype(BF16)

    @pl.when(j < nu_ref[0])
    def _():
        xp = xs_ref[...]
        lo = lax.bitcast_convert_type(xp << 16, F32).astype(BF16)
        hi = lax.bitcast_convert_type(xp & jnp.uint32(0xFFFF0000), F32).astype(BF16)
        x = jnp.concatenate([lo, hi], axis=1)
        g = _dot(x, wg_s[...]) + bg_ref[0]
        u = _dot(x, wu_s[...]) + bu_ref[0]
        gate = jnp.minimum(g, SWIGLU_LIMIT)
        up = jnp.clip(u, -SWIGLU_LIMIT, SWIGLU_LIMIT)
        hm_ref[...] = ((up + 1.0) * (gate * jax.nn.sigmoid(SWIGLU_ALPHA * gate))).astype(BF16)

    @pl.when(j >= nu_ref[0])
    def _():
        hm_ref[...] = jnp.zeros(hm_ref.shape, BF16)


def _moe_up(tile_e, n_used, xs, wgu, bgu):
    s = xs.shape[0]
    d = wgu.shape[1]
    f = wgu.shape[2] // 2
    tf = _pick(f, FF_COL_CAP, LANES)
    nf = f // tf
    nt = s // EXPERT_ROWS
    return pl.pallas_call(
        _moe_up_kernel,
        grid_spec=pltpu.PrefetchScalarGridSpec(
            num_scalar_prefetch=2,
            grid=(nf, nt),
            in_specs=[
                pl.BlockSpec((EXPERT_ROWS, d // 2), lambda n, j, te, nu: (jnp.minimum(j, nu[0] - 1), 0)),
                pl.BlockSpec((1, d, tf), lambda n, j, te, nu: (te[j], 0, n)),
                pl.BlockSpec((1, d, tf), lambda n, j, te, nu: (te[j], 0, nf + n)),
                pl.BlockSpec((1, 1, tf), lambda n, j, te, nu: (te[j], 0, n)),
                pl.BlockSpec((1, 1, tf), lambda n, j, te, nu: (te[j], 0, nf + n)),
            ],
            out_specs=pl.BlockSpec((EXPERT_ROWS, tf), lambda n, j, te, nu: (j, n)),
            scratch_shapes=[pltpu.VMEM((d, tf), BF16), pltpu.VMEM((d, tf), BF16)],
        ),
        out_shape=jax.ShapeDtypeStruct((s, f), BF16),
        compiler_params=_cparams(("arbitrary", "arbitrary")),
        name="moe_up",
    )(tile_e, n_used, xs, wgu, wgu, bgu, bgu)


def _moe_down_kernel(te_ref, nu_ref, hm_ref, wd_ref, bd_ref, y_ref, wd_s):
    j = pl.program_id(1)

    @pl.when(_expert_changed(te_ref, j))
    def _():
        wd_s[...] = wd_ref[0].astype(BF16)

    @pl.when(j < nu_ref[0])
    def _():
        y_ref[...] = _dot(hm_ref[...], wd_s[...]) + bd_ref[0]

    @pl.when(j >= nu_ref[0])
    def _():
        y_ref[...] = jnp.zeros(y_ref.shape, F32)


def _moe_down(tile_e, n_used, hm, wd, bd):
    s, f = hm.shape
    d = wd.shape[2]
    tn = _pick(d, DOWN_COL_CAP, LANES)
    nt = s // EXPERT_ROWS
    return pl.pallas_call(
        _moe_down_kernel,
        grid_spec=pltpu.PrefetchScalarGridSpec(
            num_scalar_prefetch=2,
            grid=(d // tn, nt),
            in_specs=[
                pl.BlockSpec((EXPERT_ROWS, f), lambda n, j, te, nu: (jnp.minimum(j, nu[0] - 1), 0)),
                pl.BlockSpec((1, f, tn), lambda n, j, te, nu: (te[j], 0, n)),
                pl.BlockSpec((1, 1, tn), lambda n, j, te, nu: (te[j], 0, n)),
            ],
            out_specs=pl.BlockSpec((EXPERT_ROWS, tn), lambda n, j, te, nu: (j, n)),
            scratch_shapes=[pltpu.VMEM((f, tn), BF16)],
        ),
        out_shape=jax.ShapeDtypeStruct((s, d), F32),
        compiler_params=_cparams(("arbitrary", "arbitrary")),
        name="moe_down",
    )(tile_e, n_used, hm, wd, bd)


def _moe_combine_kernel(offs_ref, e_ref, r_ref, en_ref, rn_ref, h1_ref, rw_ref, g2_ref, b2_ref, y_ref,
                        out_ref, ybuf, sems, *, alpha):
    i = pl.program_id(0)
    tm = h1_ref.shape[0]
    slot = i % 2

    def gather(eb_ref, rb_ref, dst_slot):
        def issue(t, carry):
            for k in range(TOP_K):
                j = t * TOP_K + k
                src = offs_ref[eb_ref[j]] + rb_ref[j]
                pltpu.make_async_copy(y_ref.at[pl.ds(src, 1)], ybuf.at[dst_slot, k, pl.ds(t, 1)],
                                      sems.at[dst_slot]).start()
            return carry

        lax.fori_loop(0, tm, issue, 0)

    @pl.when(i == 0)
    def _():
        gather(e_ref, r_ref, 0)

    @pl.when(i + 1 < pl.num_programs(0))
    def _():
        gather(en_ref, rn_ref, 1 - slot)

    for k in range(TOP_K):
        pltpu.make_async_copy(y_ref.at[pl.ds(0, tm)], ybuf.at[slot, k], sems.at[slot]).wait()
    rw = rw_ref[...]
    acc = alpha * h1_ref[...]
    for k in range(TOP_K):
        acc = acc + rw[:, k:k + 1] * ybuf[slot, k]
    out_ref[...] = _layer_norm(acc, g2_ref[...], b2_ref[...])


def _moe_combine(offs, eidx, rank, h1, rw, ln2_g, ln2_b, y, *, alpha):
    t, d = h1.shape
    tm = _pick(t, GATHER_ROWS, SUBLANES)
    nblk = t // tm
    cur = lambda i, o: (i,)
    nxt = lambda i, o: (jnp.minimum(i + 1, nblk - 1),)
    smem = lambda imap: pl.BlockSpec((tm * TOP_K,), imap, memory_space=pltpu.SMEM)
    return pl.pallas_call(
        functools.partial(_moe_combine_kernel, alpha=alpha),
        grid_spec=pltpu.PrefetchScalarGridSpec(
            num_scalar_prefetch=1,
            grid=(nblk,),
            in_specs=[
                smem(cur), smem(cur), smem(nxt), smem(nxt),
                pl.BlockSpec((tm, d), lambda i, o: (i, 0)),
                pl.BlockSpec((tm, LANES), lambda i, o: (i, 0)),
                pl.BlockSpec((1, d), lambda i, o: (0, 0)),
                pl.BlockSpec((1, d), lambda i, o: (0, 0)),
                pl.BlockSpec(memory_space=pl.ANY),
            ],
            out_specs=pl.BlockSpec((tm, d), lambda i, o: (i, 0)),
            scratch_shapes=[pltpu.VMEM((2, TOP_K, tm, d), F32), pltpu.SemaphoreType.DMA((2,))],
        ),
        out_shape=jax.ShapeDtypeStruct((t, d), F32),
        compiler_params=_cparams(("arbitrary",)),
        name="moe_combine",
    )(offs, eidx, rank, eidx, rank, h1, rw, ln2_g, ln2_b, y)


def _rope_tables(pos):
    half = D_ROPE // 2
    inv_freq = ROPE_THETA ** (-jnp.arange(half, dtype=F32) / half)
    ang = pos.astype(F32)[:, None] * inv_freq[None, :]
    zeros = jnp.zeros((pos.shape[0], LANES - D_ROPE), F32)
    cos, sin = jnp.cos(ang), jnp.sin(ang)
    return jnp.concatenate([cos, cos, zeros], axis=1), jnp.concatenate([sin, sin, zeros], axis=1)


def _rot_cols(w):
    half = D_ROPE // 2
    return jnp.concatenate([-w[..., half:], w[..., :half]], axis=-1)


def _layer(d_idx, x_p, x_s, seq, past_len, cache_kv_latent, cache_k_rope, state_conv, page_flat, ln0_g, ln0_b,
           w_in, g_q_norm, w_q_up, g_kv_norm, w_kv_up, w_attn_proj, conv_w, w_conv_proj, w_out, ln1_g, ln1_b,
           router_w, router_b, w_gate_up, b_gate_up, w_down, b_down, ln2_g, ln2_b, *, alpha):
    tp, d = x_p.shape
    db = x_s.shape[0]
    ql, kvl = g_q_norm.shape[-1], g_kv_norm.shape[-1]
    h = N_HEADS
    e = router_w.shape[-1]
    page = cache_kv_latent.shape[2]
    n_pages = past_len // page

    w_in_d = w_in[d_idx]
    o = ql + kvl
    w_kpe = w_in_d[:, o:o + D_ROPE]
    zpad = jnp.zeros((d, LANES - D_ROPE), F32)
    wa = jnp.concatenate([w_in_d[:, :o], w_kpe, zpad, _rot_cols(w_kpe), zpad], axis=1).astype(BF16)
    dc = conv_w.shape[-1]
    assert w_in_d.shape[1] == o + D_ROPE + 3 * dc + 2 * d and dc == d
    w5 = w_in_d[:, o + D_ROPE:].reshape(d, 5, d).transpose(1, 0, 2).astype(BF16)
    wq = w_q_up[d_idx].reshape(ql, h, D_QK)
    zq = jnp.zeros((ql, h, LANES - D_ROPE), F32)
    wq3 = jnp.concatenate([wq, zq], axis=2).transpose(1, 0, 2).astype(BF16)
    wqr3 = jnp.concatenate([_rot_cols(wq[:, :, D_NOPE:]), zq], axis=2).transpose(1, 0, 2).astype(BF16)
    wkv = w_kv_up[d_idx].reshape(kvl, h, D_NOPE + D_V)
    wk3 = wkv[:, :, :D_NOPE].transpose(1, 0, 2).astype(BF16)
    wv3 = wkv[:, :, D_NOPE:].transpose(1, 0, 2).astype(BF16)
    wuk3 = wkv[:, :, :D_NOPE].transpose(1, 2, 0).astype(BF16)
    gq = g_q_norm[d_idx][None]
    gkv = g_kv_norm[d_idx][None]
    wap = w_attn_proj[d_idx].astype(BF16)
    wcp = w_conv_proj[d_idx].astype(BF16)
    wout = w_out[d_idx].astype(BF16)
    rw_pad = jnp.pad(router_w[d_idx], ((0, 0), (0, LANES - e)))
    rwh = rw_pad.astype(BF16)
    rwl = (rw_pad - rwh.astype(F32)).astype(BF16)
    rb = jnp.pad(router_b[d_idx], (0, LANES - e))[None]
    wgu = w_gate_up[d_idx]
    bgu = b_gate_up[d_idx][:, None, :]
    wd = w_down[d_idx]
    bd = b_down[d_idx][:, None, :]
    g1, b1 = ln1_g[d_idx][None], ln1_b[d_idx][None]
    g2, b2 = ln2_g[d_idx][None], ln2_b[d_idx][None]
    c_p, s_p = _rope_tables(jnp.arange(seq))
    c_s, s_s = _rope_tables(jnp.full((1,), past_len))
    cw = conv_w[d_idx]
    st = state_conv[d_idx]

    xn_p, ckv_p, kpe_p, qn_p, ckvb_p = _ln_in_proj(x_p, ln0_g, ln0_b, wa, gq, gkv, c_p, s_p, seq=seq)
    xn_s, ckv_s, kpe_s, qabs_t, qpe_t = _sample_in_proj(x_s, ln0_g, ln0_b, wa, gq, gkv, c_s, s_s, wq3, wqr3, wuk3)
    yb_p, sa_p, sc_p, conv_p = _conv_gate_prompt(xn_p, w5, cw, seq=seq)
    yb_s, sa_s, sc_s, u_s = _conv_gate_sample(xn_s, w5, cw, st[:, 0], st[:, 1])

    att_p = _prompt_attn(qn_p, ckvb_p, kpe_p, c_p, s_p, wq3, wqr3, wk3, wv3, seq=seq)
    o_lat = _sample_attn(page_flat, qabs_t.transpose(1, 0, 2), qpe_t.transpose(1, 0, 2),
                         ckv_s[:, None, :], kpe_s[:, None, :],
                         cache_kv_latent[d_idx], jnp.swapaxes(cache_k_rope[d_idx], 1, 2), n_pages=n_pages)
    att_s = _sample_value(o_lat.transpose(1, 0, 2), wv3)

    m_p = _merge_proj(att_p, yb_p, sa_p, sc_p, wap, wcp)
    m_s = _merge_proj(att_s, yb_s, sa_s, sc_s, wap, wcp)
    router = functools.partial(_out_ln_router, alpha=alpha, n_experts=e)
    h1_p, hp_p, ri_p, rw_p, cnt_p = router(x_p, ln0_g, ln0_b, m_p, wout, g1, b1, rwh, rwl, rb,
                                           jnp.zeros((1, LANES), F32))
    h1_s, hp_s, ri_s, rw_s, cnt_all = router(x_s, ln0_g, ln0_b, m_s, wout, g1, b1, rwh, rwl, rb, cnt_p)

    cnt = cnt_all[0, :e].astype(jnp.int32)
    tiles = (cnt + EXPERT_ROWS - 1) // EXPERT_ROWS
    tile_end = jnp.cumsum(tiles)
    offs = ((tile_end - tiles) * EXPERT_ROWS).astype(jnp.int32)
    n_tiles = -(-((tp + db) * TOP_K) // EXPERT_ROWS) + e
    n_used = tile_end[-1:].astype(jnp.int32)
    tile_e = jnp.sum(jnp.arange(n_tiles)[:, None] >= tile_end[None, :], axis=1)
    tile_e = jnp.minimum(tile_e, jnp.take(tile_e, n_used[0] - 1)).astype(jnp.int32)
    split = lambda ri: (ri[:, :TOP_K].reshape(-1), ri[:, TOP_K:2 * TOP_K].reshape(-1))
    e_p, r_p = split(ri_p)
    e_s, r_s = split(ri_s)
    xs = jnp.zeros((n_tiles * EXPERT_ROWS, d // 2), jnp.uint32)
    xs = _moe_scatter(offs, e_p, r_p, hp_p, xs)
    xs = _moe_scatter(offs, e_s, r_s, hp_s, xs)
    hm = _moe_up(tile_e, n_used, xs, wgu, bgu)
    y = _moe_down(tile_e, n_used, hm, wd, bd)
    out_p = _moe_combine(offs, e_p, r_p, h1_p, rw_p, g2, b2, y, alpha=alpha)
    out_s = _moe_combine(offs, e_s, r_s, h1_s, rw_s, g2, b2, y, alpha=alpha)

    conv_s = jnp.stack([st[:, 1], u_s], axis=1)
    return out_p, out_s, ckv_p, kpe_p[:, :D_ROPE], conv_p, ckv_s, kpe_s[:, :D_ROPE], conv_s


def kernel(x_prompt, x_sample, cache_kv_latent, cache_k_rope, state_conv, page_table, meta_tokens, ln0_g, ln0_b,
           w_in, g_q_norm, w_q_up, g_kv_norm, w_kv_up, w_attn_proj, conv_w, w_conv_proj, w_out, ln1_g, ln1_b,
           router_w, router_b, w_gate_up, b_gate_up, w_down, b_down, ln2_g, ln2_b):
    nb, s_len, d = x_prompt.shape
    db, dec_seq, _ = x_sample.shape
    depth = w_in.shape[0]
    assert dec_seq == 1 and depth == 1
    n_meta = meta_tokens.shape[0]
    seq = s_len + n_meta
    tp = nb * seq
    past_len = page_table.shape[1] * cache_kv_latent.shape[2]
    alpha = (2 * depth) ** 0.25
    meta = jnp.broadcast_to(meta_tokens[None].astype(x_prompt.dtype), (nb, n_meta, d))
    x_p = jnp.concatenate([meta, x_prompt], axis=1).reshape(tp, d)
    x_s = x_sample.reshape(db, d)
    out_p, out_s, ckv_p, kpe_p, conv_p, ckv_s, kpe_s, conv_s = _layer(
        0, x_p, x_s, seq, past_len, cache_kv_latent, cache_k_rope, state_conv, page_table.reshape(-1),
        ln0_g[None], ln0_b[None], w_in, g_q_norm, w_q_up, g_kv_norm, w_kv_up, w_attn_proj, conv_w,
        w_conv_proj, w_out, ln1_g, ln1_b, router_w, router_b, w_gate_up, b_gate_up, w_down, b_down,
        ln2_g, ln2_b, alpha=alpha)
    kvl = ckv_p.shape[-1]
    return (out_p.reshape(nb, seq, d)[:, n_meta:], out_s.reshape(db, 1, d),
            ckv_p.reshape(1, nb, seq, kvl), kpe_p.reshape(1, nb, seq, D_ROPE), conv_p[None],
            ckv_s.reshape(1, db, 1, kvl), kpe_s.reshape(1, db, 1, D_ROPE), conv_s[None])
```

```python
import functools
import math

import jax
import jax.numpy as jnp
from jax import lax
from jax.experimental import pallas as pl
from jax.experimental.pallas import tpu as pltpu

F32 = jnp.float32
BF16 = jnp.bfloat16

N_HEADS = 16
D_NOPE = 128
D_ROPE = 64
D_V = 128
D_QK = D_NOPE + D_ROPE
ROPE_THETA = 10000.0
CONV_W = 3
TOP_K = 4
SWIGLU_LIMIT = 7.0
SWIGLU_ALPHA = 1.702
LN_EPS = 1e-5
RMS_EPS = 1e-6
NEG_INF = -1e30

LANES = 128
SUBLANES = 8
BF16_ROWS = 16
VMEM_LIMIT = 56 * 1024 * 1024

ROW_CAP = 704
CONV_COL_CAP = 512
MERGE_COL_CAP = 1024
ATTN_Q_ROWS = 512
PAGES_PER_STEP = 16
EXPERT_ROWS = 256
FF_COL_CAP = 1024
DOWN_COL_CAP = 2048
GATHER_ROWS = 128


def _pick(n, cap, mult):
    best = None
    for c in range(mult, min(n, cap) + 1, mult):
        if n % c == 0:
            best = c
    if best is None:
        raise ValueError(f"no block of {n} that is a multiple of {mult} and <= {cap}")
    return best


def _cparams(sem):
    return pltpu.CompilerParams(dimension_semantics=sem, vmem_limit_bytes=VMEM_LIMIT)


def _layer_norm(x, g, b):
    mu = jnp.mean(x, axis=-1, keepdims=True)
    xc = x - mu
    var = jnp.mean(xc * xc, axis=-1, keepdims=True)
    return xc * lax.rsqrt(var + LN_EPS) * g + b


def _rms_norm(x, g):
    return x * lax.rsqrt(jnp.mean(x * x, axis=-1, keepdims=True) + RMS_EPS) * g


def _dot(a, b):
    return jnp.dot(a, b, preferred_element_type=F32)


def _dot_nt(a, b):
    return lax.dot_general(a, b, (((1,), (1,)), ((), ())), preferred_element_type=F32)


def _ln_in_proj_kernel(x_ref, g0_ref, b0_ref, wa_ref, gq_ref, gkv_ref, c_ref, s_ref,
                       xn_ref, ckv_ref, kpe_ref, qn_ref, ckvb_ref, *, ql, kvl):
    xn = _layer_norm(x_ref[...], g0_ref[...], b0_ref[...]).astype(BF16)
    xn_ref[...] = xn
    za = _dot(xn, wa_ref[...])
    ckv = _rms_norm(za[:, ql:ql + kvl], gkv_ref[...])
    ckv_ref[...] = ckv
    ckvb_ref[...] = ckv.astype(BF16)
    o = ql + kvl
    kpe_ref[...] = za[:, o:o + LANES] * c_ref[...] + za[:, o + LANES:o + 2 * LANES] * s_ref[...]
    qn_ref[...] = _rms_norm(za[:, :ql], gq_ref[...]).astype(BF16)


def _ln_in_proj(x, ln0_g, ln0_b, wa, gq, gkv, c128, s128, *, seq):
    tp, d = x.shape
    ql, kvl = gq.shape[1], gkv.shape[1]
    tm = _pick(seq, ROW_CAP, BF16_ROWS)
    nsb = seq // tm
    row = lambda i: (i, 0)
    const = lambda i: (0, 0)
    return pl.pallas_call(
        functools.partial(_ln_in_proj_kernel, ql=ql, kvl=kvl),
        grid=(tp // tm,),
        in_specs=[
            pl.BlockSpec((tm, d), row),
            pl.BlockSpec((1, d), const),
            pl.BlockSpec((1, d), const),
            pl.BlockSpec(wa.shape, const),
            pl.BlockSpec((1, ql), const),
            pl.BlockSpec((1, kvl), const),
            pl.BlockSpec((tm, LANES), lambda i: (i % nsb, 0)),
            pl.BlockSpec((tm, LANES), lambda i: (i % nsb, 0)),
        ],
        out_specs=[
            pl.BlockSpec((tm, d), row),
            pl.BlockSpec((tm, kvl), row),
            pl.BlockSpec((tm, LANES), row),
            pl.BlockSpec((tm, ql), row),
            pl.BlockSpec((tm, kvl), row),
        ],
        out_shape=[
            jax.ShapeDtypeStruct((tp, d), BF16),
            jax.ShapeDtypeStruct((tp, kvl), F32),
            jax.ShapeDtypeStruct((tp, LANES), F32),
            jax.ShapeDtypeStruct((tp, ql), BF16),
            jax.ShapeDtypeStruct((tp, kvl), BF16),
        ],
        compiler_params=_cparams(("parallel",)),
        name="ln_in_proj",
    )(x, ln0_g, ln0_b, wa, gq, gkv, c128, s128)


def _sample_in_proj_kernel(x_ref, g0_ref, b0_ref, wa_ref, gq_ref, gkv_ref, c_ref, s_ref,
                           wq_ref, wqr_ref, wuk_ref,
                           xn_ref, ckv_ref, kpe_ref, qabs_ref, qpe_ref, *, ql, kvl, n_heads):
    xn = _layer_norm(x_ref[...], g0_ref[...], b0_ref[...]).astype(BF16)
    xn_ref[...] = xn
    za = _dot(xn, wa_ref[...])
    ckv_ref[...] = _rms_norm(za[:, ql:ql + kvl], gkv_ref[...])
    o = ql + kvl
    c = c_ref[...]
    s = s_ref[...]
    kpe_ref[...] = za[:, o:o + LANES] * c + za[:, o + LANES:o + 2 * LANES] * s
    qn = _rms_norm(za[:, :ql], gq_ref[...]).astype(BF16)
    for h in range(n_heads):
        qa = _dot(qn, wq_ref[h])
        qb = _dot(qn, wqr_ref[h])
        qpe_ref[h] = (qa[:, D_NOPE:] * c + qb * s).astype(BF16)
        qabs_ref[h] = _dot(qa[:, :D_NOPE].astype(BF16), wuk_ref[h]).astype(BF16)


def _sample_in_proj(x, ln0_g, ln0_b, wa, gq, gkv, c128, s128, wq3, wqr3, wuk3):
    db, d = x.shape
    ql, kvl = gq.shape[1], gkv.shape[1]
    h = wq3.shape[0]
    const2 = lambda i: (0, 0)
    const3 = lambda i: (0, 0, 0)
    return pl.pallas_call(
        functools.partial(_sample_in_proj_kernel, ql=ql, kvl=kvl, n_heads=h),
        grid=(1,),
        in_specs=[
            pl.BlockSpec((db, d), const2),
            pl.BlockSpec((1, d), const2),
            pl.BlockSpec((1, d), const2),
            pl.BlockSpec(wa.shape, const2),
            pl.BlockSpec((1, ql), const2),
            pl.BlockSpec((1, kvl), const2),
            pl.BlockSpec((1, LANES), const2),
            pl.BlockSpec((1, LANES), const2),
            pl.BlockSpec(wq3.shape, const3),
            pl.BlockSpec(wqr3.shape, const3),
            pl.BlockSpec(wuk3.shape, const3),
        ],
        out_specs=[
            pl.BlockSpec((db, d), const2),
            pl.BlockSpec((db, kvl), const2),
            pl.BlockSpec((db, LANES), const2),
            pl.BlockSpec((h, db, kvl), const3),
            pl.BlockSpec((h, db, LANES), const3),
        ],
        out_shape=[
            jax.ShapeDtypeStruct((db, d), BF16),
            jax.ShapeDtypeStruct((db, kvl), F32),
            jax.ShapeDtypeStruct((db, LANES), F32),
            jax.ShapeDtypeStruct((h, db, kvl), BF16),
            jax.ShapeDtypeStruct((h, db, LANES), BF16),
        ],
        compiler_params=_cparams(("arbitrary",)),
        name="sample_in_proj",
    )(x, ln0_g, ln0_b, wa, gq, gkv, c128, s128, wq3, wqr3, wuk3)


def _conv_gate_prompt_kernel(xn_ref, w_ref, cw_ref, yb_ref, sa_ref, sc_ref, cst_ref, halo_ref, *, nsb):
    i = pl.program_id(1)
    xn = xn_ref[...]
    tm = xn.shape[0]
    u = _dot(xn, w_ref[1]) * _dot(xn, w_ref[2])

    @pl.when((i % nsb) == 0)
    def _():
        halo_ref[...] = jnp.zeros(halo_ref.shape, F32)

    prev = halo_ref[...]
    p1 = prev[SUBLANES - 1:SUBLANES]
    p2 = prev[SUBLANES - 2:SUBLANES - 1]
    row = lax.broadcasted_iota(jnp.int32, (tm, 1), 0)
    u1 = jnp.where(row == 0, p1, pltpu.roll(u, 1, 0))
    u2 = jnp.where(row == 0, p2, jnp.where(row == 1, p1, pltpu.roll(u, 2, 0)))
    cw = cw_ref[...]
    y = cw[0:1] * u2 + cw[1:2] * u1 + cw[2:3] * u
    halo_ref[...] = u[tm - SUBLANES:]
    cst_ref[0] = u[tm - (CONV_W - 1):]
    yb_ref[...] = (_dot(xn, w_ref[0]) * y).astype(BF16)
    sa_ref[...] = jax.nn.sigmoid(_dot(xn, w_ref[3]))
    sc_ref[...] = jax.nn.sigmoid(_dot(xn, w_ref[4]))


def _conv_gate_prompt(xn, w5, conv_w, *, seq):
    tp, d = xn.shape
    dc = w5.shape[2]
    tm = _pick(seq, ROW_CAP, BF16_ROWS)
    tn = _pick(dc, CONV_COL_CAP, LANES)
    nsb = seq // tm
    nb = tp // seq
    blk = lambda j, i: (i, j)
    return pl.pallas_call(
        functools.partial(_conv_gate_prompt_kernel, nsb=nsb),
        grid=(dc // tn, tp // tm),
        in_specs=[
            pl.BlockSpec((tm, d), lambda j, i: (i, 0)),
            pl.BlockSpec((5, d, tn), lambda j, i: (0, 0, j)),
            pl.BlockSpec((CONV_W, tn), lambda j, i: (0, j)),
        ],
        out_specs=[
            pl.BlockSpec((tm, tn), blk),
            pl.BlockSpec((tm, tn), blk),
            pl.BlockSpec((tm, tn), blk),
            pl.BlockSpec((1, CONV_W - 1, tn), lambda j, i: (i // nsb, 0, j)),
        ],
        out_shape=[
            jax.ShapeDtypeStruct((tp, dc), BF16),
            jax.ShapeDtypeStruct((tp, dc), F32),
            jax.ShapeDtypeStruct((tp, dc), F32),
            jax.ShapeDtypeStruct((nb, CONV_W - 1, dc), F32),
        ],
        scratch_shapes=[pltpu.VMEM((SUBLANES, tn), F32)],
        compiler_params=_cparams(("parallel", "arbitrary")),
        name="conv_gate_prompt",
    )(xn, w5, conv_w)


def _conv_gate_sample_kernel(xn_ref, w_ref, cw_ref, s0_ref, s1_ref, yb_ref, sa_ref, sc_ref, u_ref):
    xn = xn_ref[...]
    u = _dot(xn, w_ref[1]) * _dot(xn, w_ref[2])
    cw = cw_ref[...]
    y = cw[0:1] * s0_ref[...] + cw[1:2] * s1_ref[...] + cw[2:3] * u
    u_ref[...] = u
    yb_ref[...] = (_dot(xn, w_ref[0]) * y).astype(BF16)
    sa_ref[...] = jax.nn.sigmoid(_dot(xn, w_ref[3]))
    sc_ref[...] = jax.nn.sigmoid(_dot(xn, w_ref[4]))


def _conv_gate_sample(xn, w5, conv_w, st0, st1):
    db, d = xn.shape
    dc = w5.shape[2]
    tn = _pick(dc, CONV_COL_CAP, LANES)
    col = lambda j: (0, j)
    return pl.pallas_call(
        _conv_gate_sample_kernel,
        grid=(dc // tn,),
        in_specs=[
            pl.BlockSpec((db, d), lambda j: (0, 0)),
            pl.BlockSpec((5, d, tn), lambda j: (0, 0, j)),
            pl.BlockSpec((CONV_W, tn), col),
            pl.BlockSpec((db, tn), col),
            pl.BlockSpec((db, tn), col),
        ],
        out_specs=[pl.BlockSpec((db, tn), col)] * 4,
        out_shape=[
            jax.ShapeDtypeStruct((db, dc), BF16),
            jax.ShapeDtypeStruct((db, dc), F32),
            jax.ShapeDtypeStruct((db, dc), F32),
            jax.ShapeDtypeStruct((db, dc), F32),
        ],
        compiler_params=_cparams(("parallel",)),
        name="conv_gate_sample",
    )(xn, w5, conv_w, st0, st1)


def _prompt_attn_kernel(qn_ref, ckvb_ref, kpe_ref, c_ref, s_ref, wq_ref, wqr_ref, wk_ref, wv_ref,
                        att_ref, q_s, k_s, v_s, *, tq, scale):
    seq = qn_ref.shape[0]
    qn = qn_ref[...]
    qa = _dot(qn, wq_ref[0])
    qb = _dot(qn, wqr_ref[0])
    q_s[:, :D_NOPE] = qa[:, :D_NOPE].astype(BF16)
    q_s[:, D_NOPE:] = (qa[:, D_NOPE:] * c_ref[...] + qb * s_ref[...]).astype(BF16)
    ckvb = ckvb_ref[...]
    k_s[:, :D_NOPE] = _dot(ckvb, wk_ref[0]).astype(BF16)
    k_s[:, D_NOPE:] = kpe_ref[...].astype(BF16)
    v_s[...] = _dot(ckvb, wv_ref[0]).astype(BF16)
    for q0 in range(0, seq, tq):
        q1 = min(q0 + tq, seq)
        s = _dot_nt(q_s[q0:q1, :], k_s[:q1, :]) * scale
        qpos = q0 + lax.broadcasted_iota(jnp.int32, (q1 - q0, 1), 0)
        kpos = lax.broadcasted_iota(jnp.int32, (1, q1), 1)
        s = jnp.where(kpos <= qpos, s, NEG_INF)
        p = jnp.exp(s - jnp.max(s, axis=-1, keepdims=True))
        l = jnp.sum(p, axis=-1, keepdims=True)
        o = _dot(p.astype(BF16), v_s[:q1, :])
        att_ref[q0:q1, :] = (o / l).astype(BF16)


def _prompt_attn(qn, ckvb, kpe128, c128, s128, wq3, wqr3, wk3, wv3, *, seq):
    tp, ql = qn.shape
    kvl = ckvb.shape[1]
    h = wq3.shape[0]
    nb = tp // seq
    scale = 1.0 / math.sqrt(D_QK)
    per_b = lambda b, hh: (b, 0)
    const = lambda b, hh: (0, 0)
    per_h = lambda b, hh: (hh, 0, 0)
    return pl.pallas_call(
        functools.partial(_prompt_attn_kernel, tq=ATTN_Q_ROWS, scale=scale),
        grid=(nb, h),
        in_specs=[
            pl.BlockSpec((seq, ql), per_b),
            pl.BlockSpec((seq, kvl), per_b),
            pl.BlockSpec((seq, LANES), per_b),
            pl.BlockSpec((seq, LANES), const),
            pl.BlockSpec((seq, LANES), const),
            pl.BlockSpec((1, ql, 2 * LANES), per_h),
            pl.BlockSpec((1, ql, LANES), per_h),
            pl.BlockSpec((1, kvl, D_NOPE), per_h),
            pl.BlockSpec((1, kvl, D_V), per_h),
        ],
        out_specs=pl.BlockSpec((seq, D_V), lambda b, hh: (b, hh)),
        out_shape=jax.ShapeDtypeStruct((tp, h * D_V), BF16),
        scratch_shapes=[
            pltpu.VMEM((seq, 2 * LANES), BF16),
            pltpu.VMEM((seq, 2 * LANES), BF16),
            pltpu.VMEM((seq, D_V), BF16),
        ],
        compiler_params=_cparams(("parallel", "arbitrary")),
        name="prompt_attn",
    )(qn, ckvb, kpe128, c128, s128, wq3, wqr3, wk3, wv3)


def _sample_attn_kernel(pt_ref, qa_ref, qp_ref, cn_ref, kn_ref, ckv_hbm, kpe_hbm, o_ref,
                        ckv_buf, kpe_buf, sems, m_s, l_s, acc_s, *, npg, scale):
    c = pl.program_id(1)
    nchunk = pl.num_programs(1)
    g = pl.program_id(0) * nchunk + c
    total = pl.num_programs(0) * nchunk
    slot = g % 2

    def fetch(step, dst_slot):
        for k in range(npg):
            pg = pt_ref[step * npg + k]
            pltpu.make_async_copy(ckv_hbm.at[pg], ckv_buf.at[dst_slot, k], sems.at[dst_slot, 0]).start(priority=k % 2)
            pltpu.make_async_copy(kpe_hbm.at[pg], kpe_buf.at[dst_slot, k], sems.at[dst_slot, 1]).start(priority=k % 2)

    @pl.when(g == 0)
    def _():
        fetch(0, 0)

    @pl.when(g + 1 < total)
    def _():
        fetch(g + 1, 1 - slot)

    pltpu.make_async_copy(ckv_hbm.at[pl.ds(0, npg)], ckv_buf.at[slot], sems.at[slot, 0]).wait()
    pltpu.make_async_copy(kpe_hbm.at[pl.ds(0, npg)], kpe_buf.at[slot], sems.at[slot, 1]).wait()

    qa = qa_ref[0]
    qp = qp_ref[0][:, :D_ROPE]

    @pl.when(c == 0)
    def _():
        cn = cn_ref[0].astype(BF16).astype(F32)
        kn = kn_ref[0][:, :D_ROPE].astype(BF16).astype(F32)
        s0 = (jnp.sum(qa.astype(F32) * cn, axis=-1, keepdims=True)
              + jnp.sum(qp.astype(F32) * kn, axis=-1, keepdims=True)) * scale
        m_s[...] = jnp.broadcast_to(s0, m_s.shape)
        l_s[...] = jnp.ones(l_s.shape, F32)
        acc_s[...] = jnp.broadcast_to(cn, acc_s.shape)

    ss = []
    for p in range(npg):
        ck = ckv_buf[slot, p].astype(BF16)
        kp = kpe_buf[slot, p].astype(BF16)
        ss.append((_dot_nt(qa, ck) + _dot(qp, kp)) * scale)
    m_prev = m_s[...]
    m_cur = jnp.max(functools.reduce(jnp.maximum, ss), axis=-1, keepdims=True)
    m_new = jnp.maximum(m_prev, m_cur)
    alpha = jnp.exp(m_prev - m_new)
    acc = alpha[:, :1] * acc_s[...]
    psum = jnp.zeros(m_prev.shape, F32)
    for p in range(npg):
        pp = jnp.exp(ss[p] - m_new)
        psum = psum + pp
        acc = acc + _dot(pp.astype(BF16), ckv_buf[slot, p].astype(BF16))
    m_s[...] = m_new
    l_new = alpha * l_s[...] + jnp.sum(psum, axis=-1, keepdims=True)
    l_s[...] = l_new
    acc_s[...] = acc

    @pl.when(c == pl.num_programs(1) - 1)
    def _():
        o_ref[0] = acc / l_new[:, :1]


def _sample_attn(page_flat, qabs, qpe, ckv_new, kpe_new, cache_ckv, cache_kpe_t, *, n_pages):
    db, h, kvl = qabs.shape
    page = cache_ckv.shape[1]
    npg = min(PAGES_PER_STEP, n_pages)
    assert n_pages % npg == 0
    nchunk = n_pages // npg
    scale = 1.0 / math.sqrt(D_QK)
    per_b = lambda b, c, pt: (b, 0, 0)
    return pl.pallas_call(
        functools.partial(_sample_attn_kernel, npg=npg, scale=scale),
        grid_spec=pltpu.PrefetchScalarGridSpec(
            num_scalar_prefetch=1,
            grid=(db, nchunk),
            in_specs=[
                pl.BlockSpec((1, h, kvl), per_b),
                pl.BlockSpec((1, h, LANES), per_b),
                pl.BlockSpec((1, 1, kvl), per_b),
                pl.BlockSpec((1, 1, LANES), per_b),
                pl.BlockSpec(memory_space=pl.ANY),
                pl.BlockSpec(memory_space=pl.ANY),
            ],
            out_specs=pl.BlockSpec((1, h, kvl), per_b),
            scratch_shapes=[
                pltpu.VMEM((2, npg, page, kvl), F32),
                pltpu.VMEM((2, npg, D_ROPE, page), F32),
                pltpu.SemaphoreType.DMA((2, 2)),
                pltpu.VMEM((h, LANES), F32),
                pltpu.VMEM((h, LANES), F32),
                pltpu.VMEM((h, kvl), F32),
            ],
        ),
        out_shape=jax.ShapeDtypeStruct((db, h, kvl), F32),
        compiler_params=_cparams(("arbitrary", "arbitrary")),
        name="sample_attn",
    )(page_flat, qabs, qpe, ckv_new, kpe_new, cache_ckv, cache_kpe_t)


def _sample_value_kernel(o_ref, wuv_ref, att_ref, *, n_heads):
    for h in range(n_heads):
        att_ref[:, h * D_V:(h + 1) * D_V] = _dot(o_ref[h].astype(BF16), wuv_ref[h]).astype(BF16)


def _sample_value(o_lat_t, wuv3):
    h, db, kvl = o_lat_t.shape
    const3 = lambda i: (0, 0, 0)
    return pl.pallas_call(
        functools.partial(_sample_value_kernel, n_heads=h),
        grid=(1,),
        in_specs=[pl.BlockSpec((h, db, kvl), const3), pl.BlockSpec(wuv3.shape, const3)],
        out_specs=pl.BlockSpec((db, h * D_V), lambda i: (0, 0)),
        out_shape=jax.ShapeDtypeStruct((db, h * D_V), BF16),
        compiler_params=_cparams(("arbitrary",)),
        name="sample_value",
    )(o_lat_t, wuv3)


def _merge_proj_kernel(att_ref, yb_ref, sa_ref, sc_ref, wap_ref, wcp_ref, m_ref):
    a = _dot(att_ref[...], wap_ref[...])
    c = _dot(yb_ref[...], wcp_ref[...])
    m_ref[...] = (sa_ref[...] * a + sc_ref[...] * c).astype(BF16)


def _merge_proj(att, yb, sa, sc, wap, wcp):
    t, da = att.shape
    dc, d = wcp.shape
    tm = _pick(t, ROW_CAP, BF16_ROWS)
    tn = _pick(d, MERGE_COL_CAP, LANES)
    return pl.pallas_call(
        _merge_proj_kernel,
        grid=(d // tn, t // tm),
        in_specs=[
            pl.BlockSpec((tm, da), lambda j, i: (i, 0)),
            pl.BlockSpec((tm, dc), lambda j, i: (i, 0)),
            pl.BlockSpec((tm, tn), lambda j, i: (i, j)),
            pl.BlockSpec((tm, tn), lambda j, i: (i, j)),
            pl.BlockSpec((da, tn), lambda j, i: (0, j)),
            pl.BlockSpec((dc, tn), lambda j, i: (0, j)),
        ],
        out_specs=pl.BlockSpec((tm, tn), lambda j, i: (i, j)),
        out_shape=jax.ShapeDtypeStruct((t, d), BF16),
        compiler_params=_cparams(("parallel", "parallel")),
        name="merge_proj",
    )(att, yb, sa, sc, wap, wcp)


def _out_ln_router_kernel(x_ref, g0_ref, b0_ref, m_ref, wout_ref, g1_ref, b1_ref, rwh_ref, rwl_ref, rb_ref,
                          cin_ref, h1_ref, hp_ref, ri_ref, rw_ref, cnt_ref, carry_ref, *, alpha, n_experts):
    i = pl.program_id(0)

    @pl.when(i == 0)
    def _():
        carry_ref[...] = jnp.broadcast_to(cin_ref[...], carry_ref.shape)

    xn = _layer_norm(x_ref[...], g0_ref[...], b0_ref[...])
    h1 = _layer_norm(alpha * xn + _dot(m_ref[...], wout_ref[...]), g1_ref[...], b1_ref[...])
    h1_ref[...] = h1
    tm, d = h1.shape
    hh = h1.astype(BF16)
    hh32 = hh.astype(F32)
    bits = lax.bitcast_convert_type(hh32, jnp.uint32)
    hp_ref[...] = (bits[:, d // 2:] & jnp.uint32(0xFFFF0000)) | (bits[:, :d // 2] >> 16)
    hl = (h1 - hh32).astype(BF16)
    rwh = rwh_ref[...]
    logits = _dot(hh, rwh) + _dot(hl, rwh) + _dot(hh, rwl_ref[...]) + rb_ref[...]
    lane = lax.broadcasted_iota(jnp.int32, (tm, LANES), 1)
    work = jnp.where(lane < n_experts, logits, -jnp.inf)
    vals, hots = [], []
    sel = jnp.zeros((tm, LANES), F32)
    ri = jnp.zeros((tm, LANES), jnp.int32)
    for k in range(TOP_K):
        mx = jnp.max(work, axis=-1, keepdims=True)
        idx = jnp.min(jnp.where(work == mx, lane, LANES), axis=-1, keepdims=True)
        hot = lane == idx
        work = jnp.where(hot, -jnp.inf, work)
        vals.append(mx)
        hots.append(hot)
        sel = sel + hot.astype(F32)
        ri = jnp.where(lane == k, idx, ri)
    es = [jnp.exp(v - vals[0]) for v in vals]
    den = es[0] + es[1] + es[2] + es[3]
    rw = jnp.zeros((tm, LANES), F32)
    for k in range(TOP_K):
        rw = jnp.where(lane == k, es[k] / den, rw)
    rw_ref[...] = rw
    r_i = lax.broadcasted_iota(jnp.int32, (tm, tm), 0)
    c_i = lax.broadcasted_iota(jnp.int32, (tm, tm), 1)
    tri = (c_i < r_i).astype(BF16)
    carry = carry_ref[0:1, :]
    rank_mat = _dot(tri, sel.astype(BF16)) + carry
    for k in range(TOP_K):
        rk = jnp.sum(jnp.where(hots[k], rank_mat, 0.0), axis=-1, keepdims=True).astype(jnp.int32)
        ri = jnp.where(lane == TOP_K + k, rk, ri)
    ri_ref[...] = ri
    new_carry = carry + jnp.sum(sel, axis=0, keepdims=True)
    carry_ref[0:1, :] = new_carry
    cnt_ref[...] = new_carry


def _out_ln_router(x, ln0_g, ln0_b, m, wout, ln1_g, ln1_b, rwh, rwl, rb, cnt_in, *, alpha, n_experts):
    t, d = x.shape
    tm = _pick(t, ROW_CAP, BF16_ROWS)
    row = lambda i: (i, 0)
    const = lambda i: (0, 0)
    return pl.pallas_call(
        functools.partial(_out_ln_router_kernel, alpha=alpha, n_experts=n_experts),
        grid=(t // tm,),
        in_specs=[
            pl.BlockSpec((tm, d), row),
            pl.BlockSpec((1, d), const),
            pl.BlockSpec((1, d), const),
            pl.BlockSpec((tm, d), row),
            pl.BlockSpec((d, d), const),
            pl.BlockSpec((1, d), const),
            pl.BlockSpec((1, d), const),
            pl.BlockSpec((d, LANES), const),
            pl.BlockSpec((d, LANES), const),
            pl.BlockSpec((1, LANES), const),
            pl.BlockSpec((1, LANES), const),
        ],
        out_specs=[
            pl.BlockSpec((tm, d), row),
            pl.BlockSpec((tm, d // 2), row),
            pl.BlockSpec((tm, LANES), row),
            pl.BlockSpec((tm, LANES), row),
            pl.BlockSpec((1, LANES), const),
        ],
        out_shape=[
            jax.ShapeDtypeStruct((t, d), F32),
            jax.ShapeDtypeStruct((t, d // 2), jnp.uint32),
            jax.ShapeDtypeStruct((t, LANES), jnp.int32),
            jax.ShapeDtypeStruct((t, LANES), F32),
            jax.ShapeDtypeStruct((1, LANES), F32),
        ],
        scratch_shapes=[pltpu.VMEM((SUBLANES, LANES), F32)],
        compiler_params=_cparams(("arbitrary",)),
        name="out_ln_router",
    )(x, ln0_g, ln0_b, m, wout, ln1_g, ln1_b, rwh, rwl, rb, cnt_in)


def _moe_scatter_kernel(dest_ref, h_ref, xs_in, xs_ref, sem):
    del xs_in
    tm = h_ref.shape[0]

    def issue(t, carry):
        for k in range(TOP_K):
            dest = dest_ref[t * TOP_K + k]
            pltpu.make_async_copy(h_ref.at[pl.ds(t, 1)], xs_ref.at[pl.ds(dest, 1)], sem).start(priority=k % 2)
        return carry

    lax.fori_loop(0, tm, issue, 0)
    for _ in range(TOP_K):
        pltpu.make_async_copy(h_ref, xs_ref.at[pl.ds(0, tm)], sem).wait()


def _moe_scatter(dest, h1, xs0):
    t, d = h1.shape
    tm = _pick(t, GATHER_ROWS, SUBLANES)
    return pl.pallas_call(
        _moe_scatter_kernel,
        grid=(t // tm,),
        in_specs=[
            pl.BlockSpec((tm * TOP_K,), lambda i: (i,), memory_space=pltpu.SMEM),
            pl.BlockSpec((tm, d), lambda i: (i, 0)),
            pl.BlockSpec(memory_space=pl.ANY),
        ],
        out_specs=pl.BlockSpec(memory_space=pl.ANY),
        scratch_shapes=[pltpu.SemaphoreType.DMA],
        out_shape=jax.ShapeDtypeStruct(xs0.shape, xs0.dtype),
        input_output_aliases={2: 0},
        compiler_params=_cparams(("arbitrary",)),
        name="moe_scatter",
    )(dest, h1, xs0)


def _expert_changed(te_ref, j):
    return (j == 0) | (te_ref[j] != te_ref[jnp.maximum(j - 1, 0)])


def _moe_up_kernel(te_ref, nu_ref, xs_ref, wg_ref, wu_ref, bg_ref, bu_ref, hm_ref, wg_s, wu_s):
    j = pl.program_id(1)

    @pl.when(_expert_changed(te_ref, j))
    def _():
        wg_s[...] = wg_ref[0].astype(BF16)
        wu_s[...] = wu_ref[0].astype(BF16)

    @pl.when(j < nu_ref[0])
    def _():
        xp = xs_ref[...]
        lo = lax.bitcast_convert_type(xp << 16, F32).astype(BF16)
        hi = lax.bitcast_convert_type(xp & jnp.uint32(0xFFFF0000), F32).astype(BF16)
        x = jnp.concatenate([lo, hi], axis=1)
        g = _dot(x, wg_s[...]) + bg_ref[0]
        u = _dot(x, wu_s[...]) + bu_ref[0]
        gate = jnp.minimum(g, SWIGLU_LIMIT)
        up = jnp.clip(u, -SWIGLU_LIMIT, SWIGLU_LIMIT)
        hm_ref[...] = ((up + 1.0) * (gate * jax.nn.sigmoid(SWIGLU_ALPHA * gate))).astype(BF16)

    @pl.when(j >= nu_ref[0])
    def _():
        hm_ref[...] = jnp.zeros(hm_ref.shape, BF16)


def _moe_up(tile_e, n_used, xs, wgu, bgu):
    s = xs.shape[0]
    d = wgu.shape[1]
    f = wgu.shape[2] // 2
    tf = _pick(f, FF_COL_CAP, LANES)
    nf = f // tf
    nt = s // EXPERT_ROWS
    return pl.pallas_call(
        _moe_up_kernel,
        grid_spec=pltpu.PrefetchScalarGridSpec(
            num_scalar_prefetch=2,
            grid=(nf, nt),
            in_specs=[
                pl.BlockSpec((EXPERT_ROWS, d // 2), lambda n, j, te, nu: (jnp.minimum(j, nu[0] - 1), 0)),
                pl.BlockSpec((1, d, tf), lambda n, j, te, nu: (te[j], 0, n)),
                pl.BlockSpec((1, d, tf), lambda n, j, te, nu: (te[j], 0, nf + n)),
                pl.BlockSpec((1, 1, tf), lambda n, j, te, nu: (te[j], 0, n)),
                pl.BlockSpec((1, 1, tf), lambda n, j, te, nu: (te[j], 0, nf + n)),
            ],
            out_specs=pl.BlockSpec((EXPERT_ROWS, tf), lambda n, j, te, nu: (j, n)),
            scratch_shapes=[pltpu.VMEM((d, tf), BF16), pltpu.VMEM((d, tf), BF16)],
        ),
        out_shape=jax.ShapeDtypeStruct((s, f), BF16),
        compiler_params=_cparams(("arbitrary", "arbitrary")),
        name="moe_up",
    )(tile_e, n_used, xs, wgu, wgu, bgu, bgu)


def _moe_down_kernel(te_ref, nu_ref, hm_ref, wd_ref, bd_ref, y_ref, wd_s):
    j = pl.program_id(1)

    @pl.when(_expert_changed(te_ref, j))
    def _():
        wd_s[...] = wd_ref[0].astype(BF16)

    @pl.when(j < nu_ref[0])
    def _():
        y_ref[...] = _dot(hm_ref[...], wd_s[...]) + bd_ref[0]

    @pl.when(j >= nu_ref[0])
    def _():
        y_ref[...] = jnp.zeros(y_ref.shape, F32)


def _moe_down(tile_e, n_used, hm, wd, bd):
    s, f = hm.shape
    d = wd.shape[2]
    tn = _pick(d, DOWN_COL_CAP, LANES)
    nt = s // EXPERT_ROWS
    return pl.pallas_call(
        _moe_down_kernel,
        grid_spec=pltpu.PrefetchScalarGridSpec(
            num_scalar_prefetch=2,
            grid=(d // tn, nt),
            in_specs=[
                pl.BlockSpec((EXPERT_ROWS, f), lambda n, j, te, nu: (jnp.minimum(j, nu[0] - 1), 0)),
                pl.BlockSpec((1, f, tn), lambda n, j, te, nu: (te[j], 0, n)),
                pl.BlockSpec((1, 1, tn), lambda n, j, te, nu: (te[j], 0, n)),
            ],
            out_specs=pl.BlockSpec((EXPERT_ROWS, tn), lambda n, j, te, nu: (j, n)),
            scratch_shapes=[pltpu.VMEM((f, tn), BF16)],
        ),
        out_shape=jax.ShapeDtypeStruct((s, d), F32),
        compiler_params=_cparams(("arbitrary", "arbitrary")),
        name="moe_down",
    )(tile_e, n_used, hm, wd, bd)


def _moe_combine_kernel(dest_ref, dnext_ref, h1_ref, rw_ref, g2_ref, b2_ref, y_ref, out_ref, ybuf, sems, *, alpha):
    i = pl.program_id(0)
    tm = h1_ref.shape[0]
    slot = i % 2

    def gather(d_ref, dst_slot):
        def issue(t, carry):
            for k in range(TOP_K):
                src = d_ref[t * TOP_K + k]
                pltpu.make_async_copy(y_ref.at[pl.ds(src, 1)], ybuf.at[dst_slot, k, pl.ds(t, 1)],
                                      sems.at[dst_slot]).start(priority=k % 2)
            return carry

        lax.fori_loop(0, tm, issue, 0)

    @pl.when(i == 0)
    def _():
        gather(dest_ref, 0)

    for s in range(2):
        @pl.when((i + 1 < pl.num_programs(0)) & (slot == s))
        def _():
            gather(dnext_ref, 1 - s)

    for k in range(TOP_K):
        pltpu.make_async_copy(y_ref.at[pl.ds(0, tm)], ybuf.at[slot, k], sems.at[slot]).wait()
    rw = rw_ref[...]
    acc = alpha * h1_ref[...]
    for k in range(TOP_K):
        acc = acc + rw[:, k:k + 1] * ybuf[slot, k]
    out_ref[...] = _layer_norm(acc, g2_ref[...], b2_ref[...])


def _moe_combine(dest, h1, rw, ln2_g, ln2_b, y, *, alpha):
    t, d = h1.shape
    tm = _pick(t, GATHER_ROWS, SUBLANES)
    nblk = t // tm
    smem = lambda imap: pl.BlockSpec((tm * TOP_K,), imap, memory_space=pltpu.SMEM)
    return pl.pallas_call(
        functools.partial(_moe_combine_kernel, alpha=alpha),
        grid=(nblk,),
        in_specs=[
            smem(lambda i: (i,)),
            smem(lambda i: (jnp.minimum(i + 1, nblk - 1),)),
            pl.BlockSpec((tm, d), lambda i: (i, 0)),
            pl.BlockSpec((tm, LANES), lambda i: (i, 0)),
            pl.BlockSpec((1, d), lambda i: (0, 0)),
            pl.BlockSpec((1, d), lambda i: (0, 0)),
            pl.BlockSpec(memory_space=pl.ANY),
        ],
        out_specs=pl.BlockSpec((tm, d), lambda i: (i, 0)),
        scratch_shapes=[pltpu.VMEM((2, TOP_K, tm, d), F32), pltpu.SemaphoreType.DMA((2,))],
        out_shape=jax.ShapeDtypeStruct((t, d), F32),
        compiler_params=_cparams(("arbitrary",)),
        name="moe_combine",
    )(dest, dest, h1, rw, ln2_g, ln2_b, y)


def _rope_tables(pos):
    half = D_ROPE // 2
    inv_freq = ROPE_THETA ** (-jnp.arange(half, dtype=F32) / half)
    ang = pos.astype(F32)[:, None] * inv_freq[None, :]
    zeros = jnp.zeros((pos.shape[0], LANES - D_ROPE), F32)
    cos, sin = jnp.cos(ang), jnp.sin(ang)
    return jnp.concatenate([cos, cos, zeros], axis=1), jnp.concatenate([sin, sin, zeros], axis=1)


def _rot_cols(w):
    half = D_ROPE // 2
    return jnp.concatenate([-w[..., half:], w[..., :half]], axis=-1)


def _layer(d_idx, x_p, x_s, seq, past_len, cache_kv_latent, cache_k_rope, state_conv, page_flat, ln0_g, ln0_b,
           w_in, g_q_norm, w_q_up, g_kv_norm, w_kv_up, w_attn_proj, conv_w, w_conv_proj, w_out, ln1_g, ln1_b,
           router_w, router_b, w_gate_up, b_gate_up, w_down, b_down, ln2_g, ln2_b, *, alpha):
    tp, d = x_p.shape
    db = x_s.shape[0]
    ql, kvl = g_q_norm.shape[-1], g_kv_norm.shape[-1]
    h = N_HEADS
    e = router_w.shape[-1]
    page = cache_kv_latent.shape[2]
    n_pages = past_len // page

    w_in_d = w_in[d_idx]
    o = ql + kvl
    w_kpe = w_in_d[:, o:o + D_ROPE]
    zpad = jnp.zeros((d, LANES - D_ROPE), F32)
    wa = jnp.concatenate([w_in_d[:, :o], w_kpe, zpad, _rot_cols(w_kpe), zpad], axis=1).astype(BF16)
    dc = conv_w.shape[-1]
    assert w_in_d.shape[1] == o + D_ROPE + 3 * dc + 2 * d and dc == d
    w5 = w_in_d[:, o + D_ROPE:].reshape(d, 5, d).transpose(1, 0, 2).astype(BF16)
    wq = w_q_up[d_idx].reshape(ql, h, D_QK)
    zq = jnp.zeros((ql, h, LANES - D_ROPE), F32)
    wq3 = jnp.concatenate([wq, zq], axis=2).transpose(1, 0, 2).astype(BF16)
    wqr3 = jnp.concatenate([_rot_cols(wq[:, :, D_NOPE:]), zq], axis=2).transpose(1, 0, 2).astype(BF16)
    wkv = w_kv_up[d_idx].reshape(kvl, h, D_NOPE + D_V)
    wk3 = wkv[:, :, :D_NOPE].transpose(1, 0, 2).astype(BF16)
    wv3 = wkv[:, :, D_NOPE:].transpose(1, 0, 2).astype(BF16)
    wuk3 = wkv[:, :, :D_NOPE].transpose(1, 2, 0).astype(BF16)
    gq = g_q_norm[d_idx][None]
    gkv = g_kv_norm[d_idx][None]
    wap = w_attn_proj[d_idx].astype(BF16)
    wcp = w_conv_proj[d_idx].astype(BF16)
    wout = w_out[d_idx].astype(BF16)
    rw_pad = jnp.pad(router_w[d_idx], ((0, 0), (0, LANES - e)))
    rwh = rw_pad.astype(BF16)
    rwl = (rw_pad - rwh.astype(F32)).astype(BF16)
    rb = jnp.pad(router_b[d_idx], (0, LANES - e))[None]
    wgu = w_gate_up[d_idx]
    bgu = b_gate_up[d_idx][:, None, :]
    wd = w_down[d_idx]
    bd = b_down[d_idx][:, None, :]
    g1, b1 = ln1_g[d_idx][None], ln1_b[d_idx][None]
    g2, b2 = ln2_g[d_idx][None], ln2_b[d_idx][None]
    c_p, s_p = _rope_tables(jnp.arange(seq))
    c_s, s_s = _rope_tables(jnp.full((1,), past_len))
    cw = conv_w[d_idx]
    st = state_conv[d_idx]

    xn_p, ckv_p, kpe_p, qn_p, ckvb_p = _ln_in_proj(x_p, ln0_g, ln0_b, wa, gq, gkv, c_p, s_p, seq=seq)
    xn_s, ckv_s, kpe_s, qabs_t, qpe_t = _sample_in_proj(x_s, ln0_g, ln0_b, wa, gq, gkv, c_s, s_s, wq3, wqr3, wuk3)
    yb_p, sa_p, sc_p, conv_p = _conv_gate_prompt(xn_p, w5, cw, seq=seq)
    yb_s, sa_s, sc_s, u_s = _conv_gate_sample(xn_s, w5, cw, st[:, 0], st[:, 1])

    att_p = _prompt_attn(qn_p, ckvb_p, kpe_p, c_p, s_p, wq3, wqr3, wk3, wv3, seq=seq)
    o_lat = _sample_attn(page_flat, qabs_t.transpose(1, 0, 2), qpe_t.transpose(1, 0, 2),
                         ckv_s[:, None, :], kpe_s[:, None, :],
                         cache_kv_latent[d_idx], jnp.swapaxes(cache_k_rope[d_idx], 1, 2), n_pages=n_pages)
    att_s = _sample_value(o_lat.transpose(1, 0, 2), wv3)

    m_p = _merge_proj(att_p, yb_p, sa_p, sc_p, wap, wcp)
    m_s = _merge_proj(att_s, yb_s, sa_s, sc_s, wap, wcp)
    router = functools.partial(_out_ln_router, alpha=alpha, n_experts=e)
    h1_p, hp_p, ri_p, rw_p, cnt_p = router(x_p, ln0_g, ln0_b, m_p, wout, g1, b1, rwh, rwl, rb,
                                           jnp.zeros((1, LANES), F32))
    h1_s, hp_s, ri_s, rw_s, cnt_all = router(x_s, ln0_g, ln0_b, m_s, wout, g1, b1, rwh, rwl, rb, cnt_p)

    cnt = cnt_all[0, :e].astype(jnp.int32)
    tiles = (cnt + EXPERT_ROWS - 1) // EXPERT_ROWS
    tile_end = jnp.cumsum(tiles)
    offs = ((tile_end - tiles) * EXPERT_ROWS).astype(jnp.int32)
    n_tiles = -(-((tp + db) * TOP_K) // EXPERT_ROWS) + e
    n_used = tile_end[-1:].astype(jnp.int32)
    tile_e = jnp.sum(jnp.arange(n_tiles)[:, None] >= tile_end[None, :], axis=1)
    tile_e = jnp.minimum(tile_e, jnp.take(tile_e, n_used[0] - 1)).astype(jnp.int32)
    slots = lambda ri: (jnp.take(offs, ri[:, :TOP_K]) + ri[:, TOP_K:2 * TOP_K]).reshape(-1)
    dest_p = slots(ri_p)
    dest_s = slots(ri_s)
    xs = jnp.zeros((n_tiles * EXPERT_ROWS, d // 2), jnp.uint32)
    xs = _moe_scatter(dest_p, hp_p, xs)
    xs = _moe_scatter(dest_s, hp_s, xs)
    hm = _moe_up(tile_e, n_used, xs, wgu, bgu)
    y = _moe_down(tile_e, n_used, hm, wd, bd)
    out_p = _moe_combine(dest_p, h1_p, rw_p, g2, b2, y, alpha=alpha)
    out_s = _moe_combine(dest_s, h1_s, rw_s, g2, b2, y, alpha=alpha)

    conv_s = jnp.stack([st[:, 1], u_s], axis=1)
    return out_p, out_s, ckv_p, kpe_p[:, :D_ROPE], conv_p, ckv_s, kpe_s[:, :D_ROPE], conv_s


def kernel(x_prompt, x_sample, cache_kv_latent, cache_k_rope, state_conv, page_table, meta_tokens, ln0_g, ln0_b,
           w_in, g_q_norm, w_q_up, g_kv_norm, w_kv_up, w_attn_proj, conv_w, w_conv_proj, w_out, ln1_g, ln1_b,
           router_w, router_b, w_gate_up, b_gate_up, w_down, b_down, ln2_g, ln2_b):
    nb, s_len, d = x_prompt.shape
    db, dec_seq, _ = x_sample.shape
    depth = w_in.shape[0]
    assert dec_seq == 1 and depth == 1
    n_meta = meta_tokens.shape[0]
    seq = s_len + n_meta
    tp = nb * seq
    past_len = page_table.shape[1] * cache_kv_latent.shape[2]
    alpha = (2 * depth) ** 0.25
    meta = jnp.broadcast_to(meta_tokens[None].astype(x_prompt.dtype), (nb, n_meta, d))
    x_p = jnp.concatenate([meta, x_prompt], axis=1).reshape(tp, d)
    x_s = x_sample.reshape(db, d)
    out_p, out_s, ckv_p, kpe_p, conv_p, ckv_s, kpe_s, conv_s = _layer(
        0, x_p, x_s, seq, past_len, cache_kv_latent, cache_k_rope, state_conv, page_table.reshape(-1),
        ln0_g[None], ln0_b[None], w_in, g_q_norm, w_q_up, g_kv_norm, w_kv_up, w_attn_proj, conv_w,
        w_conv_proj, w_out, ln1_g, ln1_b, router_w, router_b, w_gate_up, b_gate_up, w_down, b_down,
        ln2_g, ln2_b, alpha=alpha)
    kvl = ckv_p.shape[-1]
    return (out_p.reshape(nb, seq, d)[:, n_meta:], out_s.reshape(db, 1, d),
            ckv_p.reshape(1, nb, seq, kvl), kpe_p.reshape(1, nb, seq, D_ROPE), conv_p[None],
            ckv_s.reshape(1, db, 1, kvl), kpe_s.reshape(1, db, 1, D_ROPE), conv_s[None])
```

```python
import functools
import math

import jax
import jax.numpy as jnp
from jax import lax
from jax.experimental import pallas as pl
from jax.experimental.pallas import tpu as pltpu

F32 = jnp.float32
BF16 = jnp.bfloat16

N_HEADS = 16
D_NOPE = 128
D_ROPE = 64
D_V = 128
D_QK = D_NOPE + D_ROPE
ROPE_THETA = 10000.0
CONV_W = 3
TOP_K = 4
SWIGLU_LIMIT = 7.0
SWIGLU_ALPHA = 1.702
LN_EPS = 1e-5
RMS_EPS = 1e-6
NEG_INF = -1e30

LANES = 128
SUBLANES = 8
BF16_ROWS = 16
VMEM_LIMIT = 56 * 1024 * 1024

ROW_CAP = 704
CONV_COL_CAP = 512
MERGE_COL_CAP = 1024
ATTN_Q_ROWS = 512
PAGES_PER_STEP = 16
EXPERT_ROWS = 256
FF_COL_CAP = 1024
DOWN_COL_CAP = 2048
GATHER_ROWS = 128


def _pick(n, cap, mult):
    best = None
    for c in range(mult, min(n, cap) + 1, mult):
        if n % c == 0:
            best = c
    if best is None:
        raise ValueError(f"no block of {n} that is a multiple of {mult} and <= {cap}")
    return best


def _cparams(sem):
    return pltpu.CompilerParams(dimension_semantics=sem, vmem_limit_bytes=VMEM_LIMIT)


def _layer_norm(x, g, b):
    mu = jnp.mean(x, axis=-1, keepdims=True)
    xc = x - mu
    var = jnp.mean(xc * xc, axis=-1, keepdims=True)
    return xc * lax.rsqrt(var + LN_EPS) * g + b


def _rms_norm(x, g):
    return x * lax.rsqrt(jnp.mean(x * x, axis=-1, keepdims=True) + RMS_EPS) * g


def _dot(a, b):
    return jnp.dot(a, b, preferred_element_type=F32)


def _dot_nt(a, b):
    return lax.dot_general(a, b, (((1,), (1,)), ((), ())), preferred_element_type=F32)


def _ln_in_proj_kernel(x_ref, g0_ref, b0_ref, wa_ref, gq_ref, gkv_ref, c_ref, s_ref,
                       xn_ref, ckv_ref, kpe_ref, qn_ref, ckvb_ref, *, ql, kvl):
    xn = _layer_norm(x_ref[...], g0_ref[...], b0_ref[...]).astype(BF16)
    xn_ref[...] = xn
    za = _dot(xn, wa_ref[...])
    ckv = _rms_norm(za[:, ql:ql + kvl], gkv_ref[...])
    ckv_ref[...] = ckv
    ckvb_ref[...] = ckv.astype(BF16)
    o = ql + kvl
    kpe_ref[...] = za[:, o:o + LANES] * c_ref[...] + za[:, o + LANES:o + 2 * LANES] * s_ref[...]
    qn_ref[...] = _rms_norm(za[:, :ql], gq_ref[...]).astype(BF16)


def _ln_in_proj(x, ln0_g, ln0_b, wa, gq, gkv, c128, s128, *, seq):
    tp, d = x.shape
    ql, kvl = gq.shape[1], gkv.shape[1]
    tm = _pick(seq, ROW_CAP, BF16_ROWS)
    nsb = seq // tm
    row = lambda i: (i, 0)
    const = lambda i: (0, 0)
    return pl.pallas_call(
        functools.partial(_ln_in_proj_kernel, ql=ql, kvl=kvl),
        grid=(tp // tm,),
        in_specs=[
            pl.BlockSpec((tm, d), row),
            pl.BlockSpec((1, d), const),
            pl.BlockSpec((1, d), const),
            pl.BlockSpec(wa.shape, const),
            pl.BlockSpec((1, ql), const),
            pl.BlockSpec((1, kvl), const),
            pl.BlockSpec((tm, LANES), lambda i: (i % nsb, 0)),
            pl.BlockSpec((tm, LANES), lambda i: (i % nsb, 0)),
        ],
        out_specs=[
            pl.BlockSpec((tm, d), row),
            pl.BlockSpec((tm, kvl), row),
            pl.BlockSpec((tm, LANES), row),
            pl.BlockSpec((tm, ql), row),
            pl.BlockSpec((tm, kvl), row),
        ],
        out_shape=[
            jax.ShapeDtypeStruct((tp, d), BF16),
            jax.ShapeDtypeStruct((tp, kvl), F32),
            jax.ShapeDtypeStruct((tp, LANES), F32),
            jax.ShapeDtypeStruct((tp, ql), BF16),
            jax.ShapeDtypeStruct((tp, kvl), BF16),
        ],
        compiler_params=_cparams(("parallel",)),
        name="ln_in_proj",
    )(x, ln0_g, ln0_b, wa, gq, gkv, c128, s128)


def _sample_in_proj_kernel(x_ref, g0_ref, b0_ref, wa_ref, gq_ref, gkv_ref, c_ref, s_ref,
                           wq_ref, wqr_ref, wuk_ref,
                           xn_ref, ckv_ref, kpe_ref, qabs_ref, qpe_ref, *, ql, kvl, n_heads):
    xn = _layer_norm(x_ref[...], g0_ref[...], b0_ref[...]).astype(BF16)
    xn_ref[...] = xn
    za = _dot(xn, wa_ref[...])
    ckv_ref[...] = _rms_norm(za[:, ql:ql + kvl], gkv_ref[...])
    o = ql + kvl
    c = c_ref[...]
    s = s_ref[...]
    kpe_ref[...] = za[:, o:o + LANES] * c + za[:, o + LANES:o + 2 * LANES] * s
    qn = _rms_norm(za[:, :ql], gq_ref[...]).astype(BF16)
    for h in range(n_heads):
        qa = _dot(qn, wq_ref[h])
        qb = _dot(qn, wqr_ref[h])
        qpe_ref[h] = (qa[:, D_NOPE:] * c + qb * s).astype(BF16)
        qabs_ref[h] = _dot(qa[:, :D_NOPE].astype(BF16), wuk_ref[h]).astype(BF16)


def _sample_in_proj(x, ln0_g, ln0_b, wa, gq, gkv, c128, s128, wq3, wqr3, wuk3):
    db, d = x.shape
    ql, kvl = gq.shape[1], gkv.shape[1]
    h = wq3.shape[0]
    const2 = lambda i: (0, 0)
    const3 = lambda i: (0, 0, 0)
    return pl.pallas_call(
        functools.partial(_sample_in_proj_kernel, ql=ql, kvl=kvl, n_heads=h),
        grid=(1,),
        in_specs=[
            pl.BlockSpec((db, d), const2),
            pl.BlockSpec((1, d), const2),
            pl.BlockSpec((1, d), const2),
            pl.BlockSpec(wa.shape, const2),
            pl.BlockSpec((1, ql), const2),
            pl.BlockSpec((1, kvl), const2),
            pl.BlockSpec((1, LANES), const2),
            pl.BlockSpec((1, LANES), const2),
            pl.BlockSpec(wq3.shape, const3),
            pl.BlockSpec(wqr3.shape, const3),
            pl.BlockSpec(wuk3.shape, const3),
        ],
        out_specs=[
            pl.BlockSpec((db, d), const2),
            pl.BlockSpec((db, kvl), const2),
            pl.BlockSpec((db, LANES), const2),
            pl.BlockSpec((h, db, kvl), const3),
            pl.BlockSpec((h, db, LANES), const3),
        ],
        out_shape=[
            jax.ShapeDtypeStruct((db, d), BF16),
            jax.ShapeDtypeStruct((db, kvl), F32),
            jax.ShapeDtypeStruct((db, LANES), F32),
            jax.ShapeDtypeStruct((h, db, kvl), BF16),
            jax.ShapeDtypeStruct((h, db, LANES), BF16),
        ],
        compiler_params=_cparams(("arbitrary",)),
        name="sample_in_proj",
    )(x, ln0_g, ln0_b, wa, gq, gkv, c128, s128, wq3, wqr3, wuk3)


def _conv_gate_prompt_kernel(xn_ref, w_ref, cw_ref, yb_ref, sa_ref, sc_ref, cst_ref, halo_ref, *, nsb):
    i = pl.program_id(1)
    xn = xn_ref[...]
    tm = xn.shape[0]
    u = _dot(xn, w_ref[1]) * _dot(xn, w_ref[2])

    @pl.when((i % nsb) == 0)
    def _():
        halo_ref[...] = jnp.zeros(halo_ref.shape, F32)

    prev = halo_ref[...]
    p1 = prev[SUBLANES - 1:SUBLANES]
    p2 = prev[SUBLANES - 2:SUBLANES - 1]
    row = lax.broadcasted_iota(jnp.int32, (tm, 1), 0)
    u1 = jnp.where(row == 0, p1, pltpu.roll(u, 1, 0))
    u2 = jnp.where(row == 0, p2, jnp.where(row == 1, p1, pltpu.roll(u, 2, 0)))
    cw = cw_ref[...]
    y = cw[0:1] * u2 + cw[1:2] * u1 + cw[2:3] * u
    halo_ref[...] = u[tm - SUBLANES:]
    cst_ref[0] = u[tm - (CONV_W - 1):]
    yb_ref[...] = (_dot(xn, w_ref[0]) * y).astype(BF16)
    sa_ref[...] = jax.nn.sigmoid(_dot(xn, w_ref[3]))
    sc_ref[...] = jax.nn.sigmoid(_dot(xn, w_ref[4]))


def _conv_gate_prompt(xn, w5, conv_w, *, seq):
    tp, d = xn.shape
    dc = w5.shape[2]
    tm = _pick(seq, ROW_CAP, BF16_ROWS)
    tn = _pick(dc, CONV_COL_CAP, LANES)
    nsb = seq // tm
    nb = tp // seq
    blk = lambda j, i: (i, j)
    return pl.pallas_call(
        functools.partial(_conv_gate_prompt_kernel, nsb=nsb),
        grid=(dc // tn, tp // tm),
        in_specs=[
            pl.BlockSpec((tm, d), lambda j, i: (i, 0)),
            pl.BlockSpec((5, d, tn), lambda j, i: (0, 0, j)),
            pl.BlockSpec((CONV_W, tn), lambda j, i: (0, j)),
        ],
        out_specs=[
            pl.BlockSpec((tm, tn), blk),
            pl.BlockSpec((tm, tn), blk),
            pl.BlockSpec((tm, tn), blk),
            pl.BlockSpec((1, CONV_W - 1, tn), lambda j, i: (i // nsb, 0, j)),
        ],
        out_shape=[
            jax.ShapeDtypeStruct((tp, dc), BF16),
            jax.ShapeDtypeStruct((tp, dc), F32),
            jax.ShapeDtypeStruct((tp, dc), F32),
            jax.ShapeDtypeStruct((nb, CONV_W - 1, dc), F32),
        ],
        scratch_shapes=[pltpu.VMEM((SUBLANES, tn), F32)],
        compiler_params=_cparams(("parallel", "arbitrary")),
        name="conv_gate_prompt",
    )(xn, w5, conv_w)


def _conv_gate_sample_kernel(xn_ref, w_ref, cw_ref, s0_ref, s1_ref, yb_ref, sa_ref, sc_ref, u_ref):
    xn = xn_ref[...]
    u = _dot(xn, w_ref[1]) * _dot(xn, w_ref[2])
    cw = cw_ref[...]
    y = cw[0:1] * s0_ref[...] + cw[1:2] * s1_ref[...] + cw[2:3] * u
    u_ref[...] = u
    yb_ref[...] = (_dot(xn, w_ref[0]) * y).astype(BF16)
    sa_ref[...] = jax.nn.sigmoid(_dot(xn, w_ref[3]))
    sc_ref[...] = jax.nn.sigmoid(_dot(xn, w_ref[4]))


def _conv_gate_sample(xn, w5, conv_w, st0, st1):
    db, d = xn.shape
    dc = w5.shape[2]
    tn = _pick(dc, CONV_COL_CAP, LANES)
    col = lambda j: (0, j)
    return pl.pallas_call(
        _conv_gate_sample_kernel,
        grid=(dc // tn,),
        in_specs=[
            pl.BlockSpec((db, d), lambda j: (0, 0)),
            pl.BlockSpec((5, d, tn), lambda j: (0, 0, j)),
            pl.BlockSpec((CONV_W, tn), col),
            pl.BlockSpec((db, tn), col),
            pl.BlockSpec((db, tn), col),
        ],
        out_specs=[pl.BlockSpec((db, tn), col)] * 4,
        out_shape=[
            jax.ShapeDtypeStruct((db, dc), BF16),
            jax.ShapeDtypeStruct((db, dc), F32),
            jax.ShapeDtypeStruct((db, dc), F32),
            jax.ShapeDtypeStruct((db, dc), F32),
        ],
        compiler_params=_cparams(("parallel",)),
        name="conv_gate_sample",
    )(xn, w5, conv_w, st0, st1)


def _prompt_attn_kernel(qn_ref, ckvb_ref, kpe_ref, c_ref, s_ref, wq_ref, wqr_ref, wk_ref, wv_ref,
                        att_ref, q_s, k_s, v_s, *, tq, scale):
    seq = qn_ref.shape[0]
    qn = qn_ref[...]
    qa = _dot(qn, wq_ref[0])
    qb = _dot(qn, wqr_ref[0])
    q_s[:, :D_NOPE] = qa[:, :D_NOPE].astype(BF16)
    q_s[:, D_NOPE:] = (qa[:, D_NOPE:] * c_ref[...] + qb * s_ref[...]).astype(BF16)
    ckvb = ckvb_ref[...]
    k_s[:, :D_NOPE] = _dot(ckvb, wk_ref[0]).astype(BF16)
    k_s[:, D_NOPE:] = kpe_ref[...].astype(BF16)
    v_s[...] = _dot(ckvb, wv_ref[0]).astype(BF16)
    for q0 in range(0, seq, tq):
        q1 = min(q0 + tq, seq)
        s = _dot_nt(q_s[q0:q1, :], k_s[:q1, :]) * scale
        qpos = q0 + lax.broadcasted_iota(jnp.int32, (q1 - q0, 1), 0)
        kpos = lax.broadcasted_iota(jnp.int32, (1, q1), 1)
        s = jnp.where(kpos <= qpos, s, NEG_INF)
        p = jnp.exp(s - jnp.max(s, axis=-1, keepdims=True))
        l = jnp.sum(p, axis=-1, keepdims=True)
        o = _dot(p.astype(BF16), v_s[:q1, :])
        att_ref[q0:q1, :] = (o / l).astype(BF16)


def _prompt_attn(qn, ckvb, kpe128, c128, s128, wq3, wqr3, wk3, wv3, *, seq):
    tp, ql = qn.shape
    kvl = ckvb.shape[1]
    h = wq3.shape[0]
    nb = tp // seq
    scale = 1.0 / math.sqrt(D_QK)
    per_b = lambda b, hh: (b, 0)
    const = lambda b, hh: (0, 0)
    per_h = lambda b, hh: (hh, 0, 0)
    return pl.pallas_call(
        functools.partial(_prompt_attn_kernel, tq=ATTN_Q_ROWS, scale=scale),
        grid=(nb, h),
        in_specs=[
            pl.BlockSpec((seq, ql), per_b),
            pl.BlockSpec((seq, kvl), per_b),
            pl.BlockSpec((seq, LANES), per_b),
            pl.BlockSpec((seq, LANES), const),
            pl.BlockSpec((seq, LANES), const),
            pl.BlockSpec((1, ql, 2 * LANES), per_h),
            pl.BlockSpec((1, ql, LANES), per_h),
            pl.BlockSpec((1, kvl, D_NOPE), per_h),
            pl.BlockSpec((1, kvl, D_V), per_h),
        ],
        out_specs=pl.BlockSpec((seq, D_V), lambda b, hh: (b, hh)),
        out_shape=jax.ShapeDtypeStruct((tp, h * D_V), BF16),
        scratch_shapes=[
            pltpu.VMEM((seq, 2 * LANES), BF16),
            pltpu.VMEM((seq, 2 * LANES), BF16),
            pltpu.VMEM((seq, D_V), BF16),
        ],
        compiler_params=_cparams(("parallel", "arbitrary")),
        name="prompt_attn",
    )(qn, ckvb, kpe128, c128, s128, wq3, wqr3, wk3, wv3)


def _sample_attn_kernel(pt_ref, qa_ref, qp_ref, cn_ref, kn_ref, ckv_hbm, kpe_hbm, o_ref,
                        ckv_buf, kpe_buf, sems, m_s, l_s, acc_s, *, npg, scale):
    c = pl.program_id(1)
    nchunk = pl.num_programs(1)
    g = pl.program_id(0) * nchunk + c
    total = pl.num_programs(0) * nchunk
    slot = g % 2

    def fetch(step, dst_slot):
        for k in range(npg):
            pg = pt_ref[step * npg + k]
            pltpu.make_async_copy(ckv_hbm.at[pg], ckv_buf.at[dst_slot, k], sems.at[dst_slot, 0]).start(priority=k % 2)
            pltpu.make_async_copy(kpe_hbm.at[pg], kpe_buf.at[dst_slot, k], sems.at[dst_slot, 1]).start(priority=k % 2)

    @pl.when(g == 0)
    def _():
        fetch(0, 0)

    @pl.when(g + 1 < total)
    def _():
        fetch(g + 1, 1 - slot)

    pltpu.make_async_copy(ckv_hbm.at[pl.ds(0, npg)], ckv_buf.at[slot], sems.at[slot, 0]).wait()
    pltpu.make_async_copy(kpe_hbm.at[pl.ds(0, npg)], kpe_buf.at[slot], sems.at[slot, 1]).wait()

    qa = qa_ref[0]
    qp = qp_ref[0][:, :D_ROPE]

    @pl.when(c == 0)
    def _():
        cn = cn_ref[0].astype(BF16).astype(F32)
        kn = kn_ref[0][:, :D_ROPE].astype(BF16).astype(F32)
        s0 = (jnp.sum(qa.astype(F32) * cn, axis=-1, keepdims=True)
              + jnp.sum(qp.astype(F32) * kn, axis=-1, keepdims=True)) * scale
        m_s[...] = jnp.broadcast_to(s0, m_s.shape)
        l_s[...] = jnp.ones(l_s.shape, F32)
        acc_s[...] = jnp.broadcast_to(cn, acc_s.shape)

    ss = []
    for p in range(npg):
        ck = ckv_buf[slot, p].astype(BF16)
        kp = kpe_buf[slot, p].astype(BF16)
        ss.append((_dot_nt(qa, ck) + _dot(qp, kp)) * scale)
    m_prev = m_s[...]
    m_cur = jnp.max(functools.reduce(jnp.maximum, ss), axis=-1, keepdims=True)
    m_new = jnp.maximum(m_prev, m_cur)
    alpha = jnp.exp(m_prev - m_new)
    acc = alpha[:, :1] * acc_s[...]
    psum = jnp.zeros(m_prev.shape, F32)
    for p in range(npg):
        pp = jnp.exp(ss[p] - m_new)
        psum = psum + pp
        acc = acc + _dot(pp.astype(BF16), ckv_buf[slot, p].astype(BF16))
    m_s[...] = m_new
    l_new = alpha * l_s[...] + jnp.sum(psum, axis=-1, keepdims=True)
    l_s[...] = l_new
    acc_s[...] = acc

    @pl.when(c == pl.num_programs(1) - 1)
    def _():
        o_ref[0] = acc / l_new[:, :1]


def _sample_attn(page_flat, qabs, qpe, ckv_new, kpe_new, cache_ckv, cache_kpe_t, *, n_pages):
    db, h, kvl = qabs.shape
    page = cache_ckv.shape[1]
    npg = min(PAGES_PER_STEP, n_pages)
    assert n_pages % npg == 0
    nchunk = n_pages // npg
    scale = 1.0 / math.sqrt(D_QK)
    per_b = lambda b, c, pt: (b, 0, 0)
    return pl.pallas_call(
        functools.partial(_sample_attn_kernel, npg=npg, scale=scale),
        grid_spec=pltpu.PrefetchScalarGridSpec(
            num_scalar_prefetch=1,
            grid=(db, nchunk),
            in_specs=[
                pl.BlockSpec((1, h, kvl), per_b),
                pl.BlockSpec((1, h, LANES), per_b),
                pl.BlockSpec((1, 1, kvl), per_b),
                pl.BlockSpec((1, 1, LANES), per_b),
                pl.BlockSpec(memory_space=pl.ANY),
                pl.BlockSpec(memory_space=pl.ANY),
            ],
            out_specs=pl.BlockSpec((1, h, kvl), per_b),
            scratch_shapes=[
                pltpu.VMEM((2, npg, page, kvl), F32),
                pltpu.VMEM((2, npg, D_ROPE, page), F32),
                pltpu.SemaphoreType.DMA((2, 2)),
                pltpu.VMEM((h, LANES), F32),
                pltpu.VMEM((h, LANES), F32),
                pltpu.VMEM((h, kvl), F32),
            ],
        ),
        out_shape=jax.ShapeDtypeStruct((db, h, kvl), F32),
        compiler_params=_cparams(("arbitrary", "arbitrary")),
        name="sample_attn",
    )(page_flat, qabs, qpe, ckv_new, kpe_new, cache_ckv, cache_kpe_t)


def _sample_value_kernel(o_ref, wuv_ref, att_ref, *, n_heads):
    for h in range(n_heads):
        att_ref[:, h * D_V:(h + 1) * D_V] = _dot(o_ref[h].astype(BF16), wuv_ref[h]).astype(BF16)


def _sample_value(o_lat_t, wuv3):
    h, db, kvl = o_lat_t.shape
    const3 = lambda i: (0, 0, 0)
    return pl.pallas_call(
        functools.partial(_sample_value_kernel, n_heads=h),
        grid=(1,),
        in_specs=[pl.BlockSpec((h, db, kvl), const3), pl.BlockSpec(wuv3.shape, const3)],
        out_specs=pl.BlockSpec((db, h * D_V), lambda i: (0, 0)),
        out_shape=jax.ShapeDtypeStruct((db, h * D_V), BF16),
        compiler_params=_cparams(("arbitrary",)),
        name="sample_value",
    )(o_lat_t, wuv3)


def _merge_proj_kernel(att_ref, yb_ref, sa_ref, sc_ref, wap_ref, wcp_ref, m_ref):
    a = _dot(att_ref[...], wap_ref[...])
    c = _dot(yb_ref[...], wcp_ref[...])
    m_ref[...] = (sa_ref[...] * a + sc_ref[...] * c).astype(BF16)


def _merge_proj(att, yb, sa, sc, wap, wcp):
    t, da = att.shape
    dc, d = wcp.shape
    tm = _pick(t, ROW_CAP, BF16_ROWS)
    tn = _pick(d, MERGE_COL_CAP, LANES)
    return pl.pallas_call(
        _merge_proj_kernel,
        grid=(d // tn, t // tm),
        in_specs=[
            pl.BlockSpec((tm, da), lambda j, i: (i, 0)),
            pl.BlockSpec((tm, dc), lambda j, i: (i, 0)),
            pl.BlockSpec((tm, tn), lambda j, i: (i, j)),
            pl.BlockSpec((tm, tn), lambda j, i: (i, j)),
            pl.BlockSpec((da, tn), lambda j, i: (0, j)),
            pl.BlockSpec((dc, tn), lambda j, i: (0, j)),
        ],
        out_specs=pl.BlockSpec((tm, tn), lambda j, i: (i, j)),
        out_shape=jax.ShapeDtypeStruct((t, d), BF16),
        compiler_params=_cparams(("parallel", "parallel")),
        name="merge_proj",
    )(att, yb, sa, sc, wap, wcp)


def _out_ln_router_kernel(x_ref, g0_ref, b0_ref, m_ref, wout_ref, g1_ref, b1_ref, rwh_ref, rwl_ref, rb_ref,
                          cin_ref, h1_ref, hp_ref, ri_ref, rw_ref, cnt_ref, carry_ref, *, alpha, n_experts):
    i = pl.program_id(0)

    @pl.when(i == 0)
    def _():
        carry_ref[...] = jnp.broadcast_to(cin_ref[...], carry_ref.shape)

    xn = _layer_norm(x_ref[...], g0_ref[...], b0_ref[...])
    h1 = _layer_norm(alpha * xn + _dot(m_ref[...], wout_ref[...]), g1_ref[...], b1_ref[...])
    h1_ref[...] = h1
    tm, d = h1.shape
    hh = h1.astype(BF16)
    hh32 = hh.astype(F32)
    bits = lax.bitcast_convert_type(hh32, jnp.uint32)
    hp_ref[...] = (bits[:, d // 2:] & jnp.uint32(0xFFFF0000)) | (bits[:, :d // 2] >> 16)
    hl = (h1 - hh32).astype(BF16)
    rwh = rwh_ref[...]
    logits = _dot(hh, rwh) + _dot(hl, rwh) + _dot(hh, rwl_ref[...]) + rb_ref[...]
    lane = lax.broadcasted_iota(jnp.int32, (tm, LANES), 1)
    work = jnp.where(lane < n_experts, logits, -jnp.inf)
    vals, hots = [], []
    sel = jnp.zeros((tm, LANES), F32)
    ri = jnp.zeros((tm, LANES), jnp.int32)
    for k in range(TOP_K):
        mx = jnp.max(work, axis=-1, keepdims=True)
        idx = jnp.min(jnp.where(work == mx, lane, LANES), axis=-1, keepdims=True)
        hot = lane == idx
        work = jnp.where(hot, -jnp.inf, work)
        vals.append(mx)
        hots.append(hot)
        sel = sel + hot.astype(F32)
        ri = jnp.where(lane == k, idx, ri)
    es = [jnp.exp(v - vals[0]) for v in vals]
    den = es[0] + es[1] + es[2] + es[3]
    rw = jnp.zeros((tm, LANES), F32)
    for k in range(TOP_K):
        rw = jnp.where(lane == k, es[k] / den, rw)
    rw_ref[...] = rw
    r_i = lax.broadcasted_iota(jnp.int32, (tm, tm), 0)
    c_i = lax.broadcasted_iota(jnp.int32, (tm, tm), 1)
    tri = (c_i < r_i).astype(BF16)
    carry = carry_ref[0:1, :]
    rank_mat = _dot(tri, sel.astype(BF16)) + carry
    for k in range(TOP_K):
        rk = jnp.sum(jnp.where(hots[k], rank_mat, 0.0), axis=-1, keepdims=True).astype(jnp.int32)
        ri = jnp.where(lane == TOP_K + k, rk, ri)
    ri_ref[...] = ri
    new_carry = carry + jnp.sum(sel, axis=0, keepdims=True)
    carry_ref[0:1, :] = new_carry
    cnt_ref[...] = new_carry


def _out_ln_router(x, ln0_g, ln0_b, m, wout, ln1_g, ln1_b, rwh, rwl, rb, cnt_in, *, alpha, n_experts):
    t, d = x.shape
    tm = _pick(t, ROW_CAP, BF16_ROWS)
    row = lambda i: (i, 0)
    const = lambda i: (0, 0)
    return pl.pallas_call(
        functools.partial(_out_ln_router_kernel, alpha=alpha, n_experts=n_experts),
        grid=(t // tm,),
        in_specs=[
            pl.BlockSpec((tm, d), row),
            pl.BlockSpec((1, d), const),
            pl.BlockSpec((1, d), const),
            pl.BlockSpec((tm, d), row),
            pl.BlockSpec((d, d), const),
            pl.BlockSpec((1, d), const),
            pl.BlockSpec((1, d), const),
            pl.BlockSpec((d, LANES), const),
            pl.BlockSpec((d, LANES), const),
            pl.BlockSpec((1, LANES), const),
            pl.BlockSpec((1, LANES), const),
        ],
        out_specs=[
            pl.BlockSpec((tm, d), row),
            pl.BlockSpec((tm, d // 2), row),
            pl.BlockSpec((tm, LANES), row),
            pl.BlockSpec((tm, LANES), row),
            pl.BlockSpec((1, LANES), const),
        ],
        out_shape=[
            jax.ShapeDtypeStruct((t, d), F32),
            jax.ShapeDtypeStruct((t, d // 2), jnp.uint32),
            jax.ShapeDtypeStruct((t, LANES), jnp.int32),
            jax.ShapeDtypeStruct((t, LANES), F32),
            jax.ShapeDtypeStruct((1, LANES), F32),
        ],
        scratch_shapes=[pltpu.VMEM((SUBLANES, LANES), F32)],
        compiler_params=_cparams(("arbitrary",)),
        name="out_ln_router",
    )(x, ln0_g, ln0_b, m, wout, ln1_g, ln1_b, rwh, rwl, rb, cnt_in)


def _moe_scatter_kernel(dest_ref, h_ref, xs_in, xs_ref, sem):
    del xs_in
    tm = h_ref.shape[0]

    def issue(t, carry):
        for k in range(TOP_K):
            dest = dest_ref[t * TOP_K + k]
            pltpu.make_async_copy(h_ref.at[pl.ds(t, 1)], xs_ref.at[pl.ds(dest, 1)], sem).start(priority=k % 2)
        return carry

    lax.fori_loop(0, tm, issue, 0)
    for _ in range(TOP_K):
        pltpu.make_async_copy(h_ref, xs_ref.at[pl.ds(0, tm)], sem).wait()


def _moe_scatter(dest, h1, xs0):
    t, d = h1.shape
    tm = _pick(t, GATHER_ROWS, SUBLANES)
    return pl.pallas_call(
        _moe_scatter_kernel,
        grid=(t // tm,),
        in_specs=[
            pl.BlockSpec((tm * TOP_K,), lambda i: (i,), memory_space=pltpu.SMEM),
            pl.BlockSpec((tm, d), lambda i: (i, 0)),
            pl.BlockSpec(memory_space=pl.ANY),
        ],
        out_specs=pl.BlockSpec(memory_space=pl.ANY),
        scratch_shapes=[pltpu.SemaphoreType.DMA],
        out_shape=jax.ShapeDtypeStruct(xs0.shape, xs0.dtype),
        input_output_aliases={2: 0},
        compiler_params=_cparams(("arbitrary",)),
        name="moe_scatter",
    )(dest, h1, xs0)


def _expert_changed(te_ref, j):
    return (j == 0) | (te_ref[j] != te_ref[jnp.maximum(j - 1, 0)])


def _moe_up_kernel(te_ref, nu_ref, xs_ref, wg_ref, wu_ref, bg_ref, bu_ref, hm_ref, wg_s, wu_s):
    j = pl.program_id(1)

    @pl.when(_expert_changed(te_ref, j))
    def _():
        wg_s[...] = wg_ref[0].astype(BF16)
        wu_s[...] = wu_ref[0].astype(BF16)

    @pl.when(j < nu_ref[0])
    def _():
        x = xs_ref[...].astype(BF16)
        g = _dot(x, wg_s[...]) + bg_ref[0]
        u = _dot(x, wu_s[...]) + bu_ref[0]
        gate = jnp.minimum(g, SWIGLU_LIMIT)
        up = jnp.clip(u, -SWIGLU_LIMIT, SWIGLU_LIMIT)
        hm_ref[...] = ((up + 1.0) * (gate * jax.nn.sigmoid(SWIGLU_ALPHA * gate))).astype(BF16)

    @pl.when(j >= nu_ref[0])
    def _():
        hm_ref[...] = jnp.zeros(hm_ref.shape, BF16)


def _moe_up(tile_e, n_used, xs, wgu, bgu):
    s = xs.shape[0]
    d = wgu.shape[1]
    f = wgu.shape[2] // 2
    tf = _pick(f, FF_COL_CAP, LANES)
    nf = f // tf
    nt = s // EXPERT_ROWS
    return pl.pallas_call(
        _moe_up_kernel,
        grid_spec=pltpu.PrefetchScalarGridSpec(
            num_scalar_prefetch=2,
            grid=(nf, nt),
            in_specs=[
                pl.BlockSpec((EXPERT_ROWS, d), lambda n, j, te, nu: (jnp.minimum(j, nu[0] - 1), 0)),
                pl.BlockSpec((1, d, tf), lambda n, j, te, nu: (te[j], 0, n)),
                pl.BlockSpec((1, d, tf), lambda n, j, te, nu: (te[j], 0, nf + n)),
                pl.BlockSpec((1, 1, tf), lambda n, j, te, nu: (te[j], 0, n)),
                pl.BlockSpec((1, 1, tf), lambda n, j, te, nu: (te[j], 0, nf + n)),
            ],
            out_specs=pl.BlockSpec((EXPERT_ROWS, tf), lambda n, j, te, nu: (j, n)),
            scratch_shapes=[pltpu.VMEM((d, tf), BF16), pltpu.VMEM((d, tf), BF16)],
        ),
        out_shape=jax.ShapeDtypeStruct((s, f), BF16),
        compiler_params=_cparams(("arbitrary", "arbitrary")),
        name="moe_up",
    )(tile_e, n_used, xs, wgu, wgu, bgu, bgu)


def _moe_down_kernel(te_ref, nu_ref, hm_ref, wd_ref, bd_ref, y_ref, wd_s):
    j = pl.program_id(1)

    @pl.when(_expert_changed(te_ref, j))
    def _():
        wd_s[...] = wd_ref[0].astype(BF16)

    @pl.when(j < nu_ref[0])
    def _():
        y_ref[...] = _dot(hm_ref[...], wd_s[...]) + bd_ref[0]

    @pl.when(j >= nu_ref[0])
    def _():
        y_ref[...] = jnp.zeros(y_ref.shape, F32)


def _moe_down(tile_e, n_used, hm, wd, bd):
    s, f = hm.shape
    d = wd.shape[2]
    tn = _pick(d, DOWN_COL_CAP, LANES)
    nt = s // EXPERT_ROWS
    return pl.pallas_call(
        _moe_down_kernel,
        grid_spec=pltpu.PrefetchScalarGridSpec(
            num_scalar_prefetch=2,
            grid=(d // tn, nt),
            in_specs=[
                pl.BlockSpec((EXPERT_ROWS, f), lambda n, j, te, nu: (jnp.minimum(j, nu[0] - 1), 0)),
                pl.BlockSpec((1, f, tn), lambda n, j, te, nu: (te[j], 0, n)),
                pl.BlockSpec((1, 1, tn), lambda n, j, te, nu: (te[j], 0, n)),
            ],
            out_specs=pl.BlockSpec((EXPERT_ROWS, tn), lambda n, j, te, nu: (j, n)),
            scratch_shapes=[pltpu.VMEM((f, tn), BF16)],
        ),
        out_shape=jax.ShapeDtypeStruct((s, d), F32),
        compiler_params=_cparams(("arbitrary", "arbitrary")),
        name="moe_down",
    )(tile_e, n_used, hm, wd, bd)


def _moe_combine_kernel(dest_ref, dnext_ref, h1_ref, rw_ref, g2_ref, b2_ref, y_ref, out_ref, ybuf, sems, *, alpha):
    i = pl.program_id(0)
    tm = h1_ref.shape[0]
    slot = i % 2

    def gather(d_ref, dst_slot):
        def issue(t, carry):
            for k in range(TOP_K):
                src = d_ref[t * TOP_K + k]
                pltpu.make_async_copy(y_ref.at[pl.ds(src, 1)], ybuf.at[dst_slot, k, pl.ds(t, 1)],
                                      sems.at[dst_slot]).start(priority=k % 2)
            return carry

        lax.fori_loop(0, tm, issue, 0)

    @pl.when(i == 0)
    def _():
        gather(dest_ref, 0)

    for s in range(2):
        @pl.when((i + 1 < pl.num_programs(0)) & (slot == s))
        def _():
            gather(dnext_ref, 1 - s)

    for k in range(TOP_K):
        pltpu.make_async_copy(y_ref.at[pl.ds(0, tm)], ybuf.at[slot, k], sems.at[slot]).wait()
    rw = rw_ref[...]
    acc = alpha * h1_ref[...]
    for k in range(TOP_K):
        acc = acc + rw[:, k:k + 1] * ybuf[slot, k]
    out_ref[...] = _layer_norm(acc, g2_ref[...], b2_ref[...])


def _moe_combine(dest, h1, rw, ln2_g, ln2_b, y, *, alpha):
    t, d = h1.shape
    tm = _pick(t, GATHER_ROWS, SUBLANES)
    nblk = t // tm
    smem = lambda imap: pl.BlockSpec((tm * TOP_K,), imap, memory_space=pltpu.SMEM)
    return pl.pallas_call(
        functools.partial(_moe_combine_kernel, alpha=alpha),
        grid=(nblk,),
        in_specs=[
            smem(lambda i: (i,)),
            smem(lambda i: (jnp.minimum(i + 1, nblk - 1),)),
            pl.BlockSpec((tm, d), lambda i: (i, 0)),
            pl.BlockSpec((tm, LANES), lambda i: (i, 0)),
            pl.BlockSpec((1, d), lambda i: (0, 0)),
            pl.BlockSpec((1, d), lambda i: (0, 0)),
            pl.BlockSpec(memory_space=pl.ANY),
        ],
        out_specs=pl.BlockSpec((tm, d), lambda i: (i, 0)),
        scratch_shapes=[pltpu.VMEM((2, TOP_K, tm, d), F32), pltpu.SemaphoreType.DMA((2,))],
        out_shape=jax.ShapeDtypeStruct((t, d), F32),
        compiler_params=_cparams(("arbitrary",)),
        name="moe_combine",
    )(dest, dest, h1, rw, ln2_g, ln2_b, y)


def _rope_tables(pos):
    half = D_ROPE // 2
    inv_freq = ROPE_THETA ** (-jnp.arange(half, dtype=F32) / half)
    ang = pos.astype(F32)[:, None] * inv_freq[None, :]
    zeros = jnp.zeros((pos.shape[0], LANES - D_ROPE), F32)
    cos, sin = jnp.cos(ang), jnp.sin(ang)
    return jnp.concatenate([cos, cos, zeros], axis=1), jnp.concatenate([sin, sin, zeros], axis=1)


def _rot_cols(w):
    half = D_ROPE // 2
    return jnp.concatenate([-w[..., half:], w[..., :half]], axis=-1)


def _layer(d_idx, x_p, x_s, seq, past_len, cache_kv_latent, cache_k_rope, state_conv, page_flat, ln0_g, ln0_b,
           w_in, g_q_norm, w_q_up, g_kv_norm, w_kv_up, w_attn_proj, conv_w, w_conv_proj, w_out, ln1_g, ln1_b,
           router_w, router_b, w_gate_up, b_gate_up, w_down, b_down, ln2_g, ln2_b, *, alpha):
    tp, d = x_p.shape
    db = x_s.shape[0]
    ql, kvl = g_q_norm.shape[-1], g_kv_norm.shape[-1]
    h = N_HEADS
    e = router_w.shape[-1]
    page = cache_kv_latent.shape[2]
    n_pages = past_len // page

    w_in_d = w_in[d_idx]
    o = ql + kvl
    w_kpe = w_in_d[:, o:o + D_ROPE]
    zpad = jnp.zeros((d, LANES - D_ROPE), F32)
    wa = jnp.concatenate([w_in_d[:, :o], w_kpe, zpad, _rot_cols(w_kpe), zpad], axis=1).astype(BF16)
    dc = conv_w.shape[-1]
    assert w_in_d.shape[1] == o + D_ROPE + 3 * dc + 2 * d and dc == d
    w5 = w_in_d[:, o + D_ROPE:].reshape(d, 5, d).transpose(1, 0, 2).astype(BF16)
    wq = w_q_up[d_idx].reshape(ql, h, D_QK)
    zq = jnp.zeros((ql, h, LANES - D_ROPE), F32)
    wq3 = jnp.concatenate([wq, zq], axis=2).transpose(1, 0, 2).astype(BF16)
    wqr3 = jnp.concatenate([_rot_cols(wq[:, :, D_NOPE:]), zq], axis=2).transpose(1, 0, 2).astype(BF16)
    wkv = w_kv_up[d_idx].reshape(kvl, h, D_NOPE + D_V)
    wk3 = wkv[:, :, :D_NOPE].transpose(1, 0, 2).astype(BF16)
    wv3 = wkv[:, :, D_NOPE:].transpose(1, 0, 2).astype(BF16)
    wuk3 = wkv[:, :, :D_NOPE].transpose(1, 2, 0).astype(BF16)
    gq = g_q_norm[d_idx][None]
    gkv = g_kv_norm[d_idx][None]
    wap = w_attn_proj[d_idx].astype(BF16)
    wcp = w_conv_proj[d_idx].astype(BF16)
    wout = w_out[d_idx].astype(BF16)
    rw_pad = jnp.pad(router_w[d_idx], ((0, 0), (0, LANES - e)))
    rwh = rw_pad.astype(BF16)
    rwl = (rw_pad - rwh.astype(F32)).astype(BF16)
    rb = jnp.pad(router_b[d_idx], (0, LANES - e))[None]
    wgu = w_gate_up[d_idx]
    bgu = b_gate_up[d_idx][:, None, :]
    wd = w_down[d_idx]
    bd = b_down[d_idx][:, None, :]
    g1, b1 = ln1_g[d_idx][None], ln1_b[d_idx][None]
    g2, b2 = ln2_g[d_idx][None], ln2_b[d_idx][None]
    c_p, s_p = _rope_tables(jnp.arange(seq))
    c_s, s_s = _rope_tables(jnp.full((1,), past_len))
    cw = conv_w[d_idx]
    st = state_conv[d_idx]

    xn_p, ckv_p, kpe_p, qn_p, ckvb_p = _ln_in_proj(x_p, ln0_g, ln0_b, wa, gq, gkv, c_p, s_p, seq=seq)
    xn_s, ckv_s, kpe_s, qabs_t, qpe_t = _sample_in_proj(x_s, ln0_g, ln0_b, wa, gq, gkv, c_s, s_s, wq3, wqr3, wuk3)
    yb_p, sa_p, sc_p, conv_p = _conv_gate_prompt(xn_p, w5, cw, seq=seq)
    yb_s, sa_s, sc_s, u_s = _conv_gate_sample(xn_s, w5, cw, st[:, 0], st[:, 1])

    att_p = _prompt_attn(qn_p, ckvb_p, kpe_p, c_p, s_p, wq3, wqr3, wk3, wv3, seq=seq)
    o_lat = _sample_attn(page_flat, qabs_t.transpose(1, 0, 2), qpe_t.transpose(1, 0, 2),
                         ckv_s[:, None, :], kpe_s[:, None, :],
                         cache_kv_latent[d_idx], jnp.swapaxes(cache_k_rope[d_idx], 1, 2), n_pages=n_pages)
    att_s = _sample_value(o_lat.transpose(1, 0, 2), wv3)

    m_p = _merge_proj(att_p, yb_p, sa_p, sc_p, wap, wcp)
    m_s = _merge_proj(att_s, yb_s, sa_s, sc_s, wap, wcp)
    router = functools.partial(_out_ln_router, alpha=alpha, n_experts=e)
    h1_p, hp_p, ri_p, rw_p, cnt_p = router(x_p, ln0_g, ln0_b, m_p, wout, g1, b1, rwh, rwl, rb,
                                           jnp.zeros((1, LANES), F32))
    h1_s, hp_s, ri_s, rw_s, cnt_all = router(x_s, ln0_g, ln0_b, m_s, wout, g1, b1, rwh, rwl, rb, cnt_p)

    cnt = cnt_all[0, :e].astype(jnp.int32)
    tiles = (cnt + EXPERT_ROWS - 1) // EXPERT_ROWS
    tile_end = jnp.cumsum(tiles)
    offs = ((tile_end - tiles) * EXPERT_ROWS).astype(jnp.int32)
    n_tiles = -(-((tp + db) * TOP_K) // EXPERT_ROWS) + e
    n_used = tile_end[-1:].astype(jnp.int32)
    tile_e = jnp.sum(jnp.arange(n_tiles)[:, None] >= tile_end[None, :], axis=1)
    tile_e = jnp.minimum(tile_e, jnp.take(tile_e, n_used[0] - 1)).astype(jnp.int32)
    slots = lambda ri: (jnp.take(offs, ri[:, :TOP_K]) + ri[:, TOP_K:2 * TOP_K]).reshape(-1)
    dest_p = slots(ri_p)
    dest_s = slots(ri_s)
    xs = jnp.zeros((n_tiles * EXPERT_ROWS, d), F32)
    xs = _moe_scatter(dest_p, h1_p, xs)
    xs = _moe_scatter(dest_s, h1_s, xs)
    hm = _moe_up(tile_e, n_used, xs, wgu, bgu)
    y = _moe_down(tile_e, n_used, hm, wd, bd)
    out_p = _moe_combine(dest_p, h1_p, rw_p, g2, b2, y, alpha=alpha)
    out_s = _moe_combine(dest_s, h1_s, rw_s, g2, b2, y, alpha=alpha)

    conv_s = jnp.stack([st[:, 1], u_s], axis=1)
    return out_p, out_s, ckv_p, kpe_p[:, :D_ROPE], conv_p, ckv_s, kpe_s[:, :D_ROPE], conv_s


def kernel(x_prompt, x_sample, cache_kv_latent, cache_k_rope, state_conv, page_table, meta_tokens, ln0_g, ln0_b,
           w_in, g_q_norm, w_q_up, g_kv_norm, w_kv_up, w_attn_proj, conv_w, w_conv_proj, w_out, ln1_g, ln1_b,
           router_w, router_b, w_gate_up, b_gate_up, w_down, b_down, ln2_g, ln2_b):
    nb, s_len, d = x_prompt.shape
    db, dec_seq, _ = x_sample.shape
    depth = w_in.shape[0]
    assert dec_seq == 1 and depth == 1
    n_meta = meta_tokens.shape[0]
    seq = s_len + n_meta
    tp = nb * seq
    past_len = page_table.shape[1] * cache_kv_latent.shape[2]
    alpha = (2 * depth) ** 0.25
    meta = jnp.broadcast_to(meta_tokens[None].astype(x_prompt.dtype), (nb, n_meta, d))
    x_p = jnp.concatenate([meta, x_prompt], axis=1).reshape(tp, d)
    x_s = x_sample.reshape(db, d)
    out_p, out_s, ckv_p, kpe_p, conv_p, ckv_s, kpe_s, conv_s = _layer(
        0, x_p, x_s, seq, past_len, cache_kv_latent, cache_k_rope, state_conv, page_table.reshape(-1),
        ln0_g[None], ln0_b[None], w_in, g_q_norm, w_q_up, g_kv_norm, w_kv_up, w_attn_proj, conv_w,
        w_conv_proj, w_out, ln1_g, ln1_b, router_w, router_b, w_gate_up, b_gate_up, w_down, b_down,
        ln2_g, ln2_b, alpha=alpha)
    kvl = ckv_p.shape[-1]
    return (out_p.reshape(nb, seq, d)[:, n_meta:], out_s.reshape(db, 1, d),
            ckv_p.reshape(1, nb, seq, kvl), kpe_p.reshape(1, nb, seq, D_ROPE), conv_p[None],
            ckv_s.reshape(1, db, 1, kvl), kpe_s.reshape(1, db, 1, D_ROPE), conv_s[None])
```
